```python
import jax, jax.numpy as jnp
from jax import lax
import numpy as np

D_MODEL = 1024
BATCH = 8
SEQ = 8192
DEPTH = 4
DEC_BATCH = 16
DEC_SEQ = 16
PAST_LEN = 1024

CHUNK = 64
N_A = DEPTH // 2
N_B = DEPTH - N_A
SSM_EXPAND = 2
D_INNER = SSM_EXPAND * D_MODEL
SSM_HEAD_DIM = 64
SSM_HEADS = D_INNER // SSM_HEAD_DIM
SSM_GROUPS = 8
SSM_HEADS_PER_GROUP = SSM_HEADS // SSM_GROUPS
D_STATE = 128
CONV_W = 4
CONV_DIM = D_INNER + 2 * SSM_GROUPS * D_STATE
D_IN_PROJ = D_INNER + CONV_DIM + SSM_HEADS
SSD_CHUNK = CHUNK
SB_HEADS = 4
SB_HEAD_DIM = 128
SB_WIDTH = SB_HEADS * SB_HEAD_DIM
SB_BLOCK = 128
D_FF = 4 * D_MODEL
PLE_DIM = 256
ALPHA = (2.0 * DEPTH) ** 0.25
BETA = (8.0 * DEPTH) ** -0.25
LN_EPS = 1e-5
RMS_EPS = 1e-5

kernel_name = 'yoco_mamba2_stickbreaking_stream_step'


def layer_norm(x, g, b):
    xf = x.astype(jnp.float32)
    mu = jnp.mean(xf, axis=-1, keepdims=True)
    xc = xf - mu
    var = jnp.mean(xc * xc, axis=-1, keepdims=True)
    return (xc * lax.rsqrt(var + LN_EPS) * g.astype(jnp.float32) + b.astype(jnp.float32)).astype(x.dtype)


def causal_dwconv(u, prev, w, b):
    T = u.shape[1]
    up = jnp.concatenate([prev.astype(u.dtype), u], axis=1)
    y = up[:, 0:T] * w[0]
    for kk in range(1, CONV_W):
        y = y + up[:, kk:kk + T] * w[kk]
    return jax.nn.silu(y + b), up[:, T:]


def ssd_chunked(xs, dt, A, Bm, Cm, h0):
    Bsz, T = xs.shape[:2]
    L = min(SSD_CHUNK, T)
    nc = T // L

    def to_chunks(a):
        return jnp.moveaxis(a.reshape((Bsz, nc, L) + a.shape[2:]), 1, 0)

    causal = jnp.tril(jnp.ones((L, L), dtype=bool))

    def step(h, inp):
        xc, dtc, Bc, Cc = inp
        cum = jnp.cumsum(dtc * A, axis=1)
        seg = cum[:, :, None] - cum[:, None, :]
        decay = jnp.exp(jnp.where(causal[None, :, :, None, None], seg, -jnp.inf))
        cb = jnp.einsum('blgn,bsgn->blsg', Cc, Bc).astype(jnp.float32)
        m = cb[..., None] * decay * dtc[:, None]
        y_diag = jnp.einsum('blsgr,bsgrp->blgrp', m, xc.astype(jnp.float32))
        y_off = jnp.einsum('blgn,bgrpn->blgrp', Cc.astype(jnp.float32), h) * jnp.exp(cum)[..., None]
        w_end = jnp.exp(cum[:, -1:] - cum) * dtc
        h_new = h * jnp.exp(cum[:, -1])[..., None, None] + jnp.einsum(
            'blgn,blgr,blgrp->bgrpn', Bc.astype(jnp.float32), w_end, xc.astype(jnp.float32))
        return h_new, y_diag + y_off

    h_last, ys = lax.scan(step, h0, (to_chunks(xs), to_chunks(dt), to_chunks(Bm), to_chunks(Cm)))
    y = jnp.moveaxis(ys, 0, 1).reshape(xs.shape)
    return y, h_last


def mamba2_mixer(x, conv_prev, h_prev, w_in, conv_w, conv_b, dt_bias, A_log, d_skip, norm_g, w_out):
    Bsz, T, _ = x.shape
    G, R, P, N = SSM_GROUPS, SSM_HEADS_PER_GROUP, SSM_HEAD_DIM, D_STATE
    proj = x @ w_in
    z = proj[..., :D_INNER]
    xbc = proj[..., D_INNER:D_INNER + CONV_DIM]
    dt_raw = proj[..., D_INNER + CONV_DIM:]
    xbc, conv_new = causal_dwconv(xbc, conv_prev, conv_w, conv_b)
    xs = xbc[..., :D_INNER].reshape(Bsz, T, G, R, P)
    Bm = xbc[..., D_INNER:D_INNER + G * N].reshape(Bsz, T, G, N)
    Cm = xbc[..., D_INNER + G * N:].reshape(Bsz, T, G, N)
    dt = jax.nn.softplus(dt_raw.astype(jnp.float32) + dt_bias.astype(jnp.float32)).reshape(Bsz, T, G, R)
    A = -jnp.exp(A_log.astype(jnp.float32)).reshape(G, R)
    h0 = h_prev.astype(jnp.float32).reshape(Bsz, G, R, P, N)
    y, h_last = ssd_chunked(xs, dt, A, Bm, Cm, h0)
    y = y + d_skip.astype(jnp.float32).reshape(G, R)[:, :, None] * xs.astype(jnp.float32)
    y = y.reshape(Bsz, T, D_INNER) * jax.nn.silu(z.astype(jnp.float32))
    yg = y.reshape(Bsz, T, G, D_INNER // G)
    yg = yg * lax.rsqrt(jnp.mean(yg * yg, axis=-1, keepdims=True) + RMS_EPS)
    y = yg.reshape(Bsz, T, D_INNER) * norm_g.astype(jnp.float32)
    out = y.astype(x.dtype) @ w_out
    return out.astype(x.dtype), conv_new, h_last.reshape(Bsz, SSM_HEADS, P, N)


def stick_breaking_attention(q, k, v, q_offset):
    Bsz, Tq, H, Dh = q.shape
    Tk = k.shape[1]
    qb = min(SB_BLOCK, Tq)
    nb = Tq // qb
    kb = SB_BLOCK
    nk_total = -(-Tk // kb)
    pad = nk_total * kb - Tk
    if pad:
        k = jnp.pad(k, ((0, 0), (0, pad), (0, 0), (0, 0)))
        v = jnp.pad(v, ((0, 0), (0, pad), (0, 0), (0, 0)))
    scale = Dh ** -0.5
    idx = jnp.arange(kb)
    upper_in = (idx[:, None] > idx[None, :]).astype(jnp.float32)
    outs = []
    for i in range(nb):
        s0 = q_offset + i * qb
        nk = min(nk_total, -(-(s0 + qb) // kb))
        kk = k[:, :nk * kb].reshape(Bsz, nk, kb, H, Dh)
        vv = v[:, :nk * kb].reshape(Bsz, nk, kb, H, Dh)
        qblk = q[:, i * qb:(i + 1) * qb]
        z = jnp.einsum('bqhd,bnjhd->bhqnj', qblk, kk).astype(jnp.float32) * scale
        q_pos = s0 + jnp.arange(qb)
        key_pos = jnp.arange(nk * kb).reshape(nk, kb)
        mask = key_pos[None, :, :] < q_pos[:, None, None]
        lk = jnp.where(mask, jax.nn.log_sigmoid(-z), 0.0)
        intra = jnp.einsum('bhqnj,js->bhqns', lk, upper_in)
        bidx = jnp.arange(nk)
        upper_blk = (bidx[:, None] > bidx[None, :]).astype(jnp.float32)
        suffix = jnp.einsum('bhqn,nm->bhqm', lk.sum(-1), upper_blk)
        w = jnp.where(mask, jnp.exp(jax.nn.log_sigmoid(z) + intra + suffix[..., None]), 0.0)
        outs.append(jnp.einsum('bhqnj,bnjhd->bqhd', w, vv.astype(jnp.float32)).astype(q.dtype))
    return jnp.concatenate(outs, axis=1) if nb > 1 else outs[0]


def run_trunk(x, p, conv_prev, ssm_prev, k_past, v_past, q_offset, prm):
    Bsz, T, _ = x.shape
    conv_states = []
    ssm_states = []
    k_all = None
    v_all = None
    k_new = None
    v_new = None
    for i in range(DEPTH):
        if i < N_A:
            mix, cs, hs = mamba2_mixer(x, conv_prev[i], ssm_prev[i], prm['a_w_in'][i], prm['a_conv_w'][i],
                                       prm['a_conv_b'][i], prm['a_dt_bias'][i], prm['a_A_log'][i],
                                       prm['a_d_skip'][i], prm['a_norm_g'][i], prm['a_w_out'][i])
            conv_states.append(cs)
            ssm_states.append(hs)
        else:
            j = i - N_A
            q = (x @ prm['b_w_q'][j]).reshape(Bsz, T, SB_HEADS, SB_HEAD_DIM)
            o = stick_breaking_attention(q, k_all, v_all, q_offset)
            mix = (o.reshape(Bsz, T, SB_WIDTH) @ prm['b_w_out'][j]).astype(x.dtype)
        x = layer_norm(ALPHA * x + mix, prm['ln1_g'][i], prm['ln1_b'][i])
        h = jnp.square(jax.nn.relu(x @ prm['mlp_w1'][i]))
        x = layer_norm(ALPHA * x + (h @ prm['mlp_w2'][i]).astype(x.dtype), prm['ln2_g'][i], prm['ln2_b'][i])
        x = x + (jax.nn.sigmoid(x @ prm['ple_gate_w'][i]) * (p[i] @ prm['ple_w'][i])).astype(x.dtype)
        if i == N_A - 1:
            kv = layer_norm(x, prm['kv_norm_g'], prm['kv_norm_b']) @ prm['w_kv']
            k_new = kv[..., :SB_WIDTH].reshape(Bsz, T, SB_HEADS, SB_HEAD_DIM)
            v_new = kv[..., SB_WIDTH:].reshape(Bsz, T, SB_HEADS, SB_HEAD_DIM)
            if k_past is None:
                k_all, v_all = k_new, v_new
            else:
                k_all = jnp.concatenate([k_past.astype(k_new.dtype), k_new], axis=1)
                v_all = jnp.concatenate([v_past.astype(v_new.dtype), v_new], axis=1)
    return x, jnp.stack(ssm_states), jnp.stack(conv_states), k_new, v_new


def setup_inputs(seed: int = 0) -> dict:
    key = jax.random.key(seed)
    ks = jax.random.split(key, 32)
    f32 = jnp.float32

    def nrm(k, shape, fan_in, s=1.0):
        return jax.random.normal(k, shape, f32) * (s * fan_in ** -0.5)

    u = jax.random.uniform(ks[10], (N_A, SSM_HEADS), f32)
    dt0 = jnp.exp(u * (jnp.log(0.1) - jnp.log(0.001)) + jnp.log(0.001))
    return {
        'x_prompt': jax.random.normal(ks[0], (BATCH, SEQ, D_MODEL), f32),
        'x_sample': jax.random.normal(ks[1], (DEC_BATCH, DEC_SEQ, D_MODEL), f32),
        'p_prompt': jax.random.normal(ks[2], (DEPTH, BATCH, SEQ, PLE_DIM), f32),
        'p_sample': jax.random.normal(ks[3], (DEPTH, DEC_BATCH, DEC_SEQ, PLE_DIM), f32),
        'state_ssm': 0.5 * jax.random.normal(ks[4], (N_A, DEC_BATCH, SSM_HEADS, SSM_HEAD_DIM, D_STATE), f32),
        'state_conv': jax.random.normal(ks[5], (N_A, DEC_BATCH, CONV_W - 1, CONV_DIM), f32),
        'cache_k': jax.random.normal(ks[6], (DEC_BATCH, PAST_LEN, SB_HEADS, SB_HEAD_DIM), f32),
        'cache_v': jax.random.normal(ks[7], (DEC_BATCH, PAST_LEN, SB_HEADS, SB_HEAD_DIM), f32),
        'a_w_in': nrm(ks[8], (N_A, D_MODEL, D_IN_PROJ), D_MODEL),
        'a_conv_w': nrm(ks[9], (N_A, CONV_W, CONV_DIM), CONV_W),
        'a_conv_b': 0.02 * jax.random.normal(ks[11], (N_A, CONV_DIM), f32),
        'a_dt_bias': dt0 + jnp.log(-jnp.expm1(-dt0)),
        'a_A_log': jnp.log(jax.random.uniform(ks[12], (N_A, SSM_HEADS), f32, 1.0, 16.0)),
        'a_d_skip': 1.0 + 0.1 * jax.random.normal(ks[13], (N_A, SSM_HEADS), f32),
        'a_norm_g': 1.0 + 0.02 * jax.random.normal(ks[14], (N_A, D_INNER), f32),
        'a_w_out': nrm(ks[15], (N_A, D_INNER, D_MODEL), D_INNER, BETA),
        'kv_norm_g': 1.0 + 0.02 * jax.random.normal(ks[16], (D_MODEL,), f32),
        'kv_norm_b': 0.02 * jax.random.normal(ks[17], (D_MODEL,), f32),
        'w_kv': nrm(ks[18], (D_MODEL, 2 * SB_WIDTH), D_MODEL),
        'b_w_q': nrm(ks[19], (N_B, D_MODEL, SB_WIDTH), D_MODEL),
        'b_w_out': nrm(ks[20], (N_B, SB_WIDTH, D_MODEL), SB_WIDTH, BETA),
        'ln1_g': 1.0 + 0.02 * jax.random.normal(ks[21], (DEPTH, D_MODEL), f32),
        'ln1_b': 0.02 * jax.random.normal(ks[22], (DEPTH, D_MODEL), f32),
        'ln2_g': 1.0 + 0.02 * jax.random.normal(ks[23], (DEPTH, D_MODEL), f32),
        'ln2_b': 0.02 * jax.random.normal(ks[24], (DEPTH, D_MODEL), f32),
        'mlp_w1': nrm(ks[25], (DEPTH, D_MODEL, D_FF), D_MODEL),
        'mlp_w2': nrm(ks[26], (DEPTH, D_FF, D_MODEL), D_FF, BETA),
        'ple_w': nrm(ks[27], (DEPTH, PLE_DIM, D_MODEL), PLE_DIM),
        'ple_gate_w': nrm(ks[28], (DEPTH, D_MODEL, D_MODEL), D_MODEL),
    }


def reference(x_prompt, x_sample, p_prompt, p_sample, state_ssm, state_conv, cache_k, cache_v,
              a_w_in, a_conv_w, a_conv_b, a_dt_bias, a_A_log, a_d_skip, a_norm_g, a_w_out,
              kv_norm_g, kv_norm_b, w_kv, b_w_q, b_w_out, ln1_g, ln1_b, ln2_g, ln2_b,
              mlp_w1, mlp_w2, ple_w, ple_gate_w):
    prm = {
        'a_w_in': a_w_in, 'a_conv_w': a_conv_w, 'a_conv_b': a_conv_b, 'a_dt_bias': a_dt_bias,
        'a_A_log': a_A_log, 'a_d_skip': a_d_skip, 'a_norm_g': a_norm_g, 'a_w_out': a_w_out,
        'kv_norm_g': kv_norm_g, 'kv_norm_b': kv_norm_b, 'w_kv': w_kv,
        'b_w_q': b_w_q, 'b_w_out': b_w_out,
        'ln1_g': ln1_g, 'ln1_b': ln1_b, 'ln2_g': ln2_g, 'ln2_b': ln2_b,
        'mlp_w1': mlp_w1, 'mlp_w2': mlp_w2, 'ple_w': ple_w, 'ple_gate_w': ple_gate_w,
    }
    Bp = x_prompt.shape[0]
    conv0 = jnp.zeros((N_A, Bp, CONV_W - 1, CONV_DIM), x_prompt.dtype)
    ssm0 = jnp.zeros((N_A, Bp, SSM_HEADS, SSM_HEAD_DIM, D_STATE), jnp.float32)
    y_prompt, ssm_p, conv_p, k_p, v_p = run_trunk(x_prompt, p_prompt, conv0, ssm0, None, None, 0, prm)
    y_sample, ssm_s, conv_s, k_s, v_s = run_trunk(x_sample, p_sample, state_conv, state_ssm,
                                                  cache_k, cache_v, PAST_LEN, prm)
    return (y_prompt, y_sample, ssm_p, conv_p, k_p, v_p, ssm_s, conv_s, k_s, v_s)
```

```python
import functools

import jax
import jax.numpy as jnp
from jax import lax
from jax.experimental import pallas as pl
from jax.experimental.pallas import tpu as pltpu

F32 = jnp.float32
BF16 = jnp.bfloat16

DEPTH = 4
N_A = DEPTH // 2
SSM_HEAD_DIM = 64
SSM_GROUPS = 8
D_STATE = 128
CONV_W = 4
SB_HEAD_DIM = 128
SB_BLOCK = 128
ALPHA = (2.0 * DEPTH) ** 0.25
LN_EPS = 1e-5
RMS_EPS = 1e-5

LANES = 128
SSD_CHUNK = 128
XWIN_PAD = 8
VMEM_LIMIT = 56 * 1024 * 1024


def _const_spec(shape):
    nd = len(shape)
    return pl.BlockSpec(shape, lambda *_: (0,) * nd, pipeline_mode=pl.Buffered(1))


def _dot(a, b):
    return jnp.dot(a, b, preferred_element_type=F32)


def _dot_nt(a, b):
    return lax.dot_general(a, b, (((1,), (1,)), ((), ())), preferred_element_type=F32)


def _split3(a):
    a1 = a.astype(BF16)
    r1 = a - a1.astype(F32)
    a2 = r1.astype(BF16)
    a3 = (r1 - a2.astype(F32)).astype(BF16)
    return a1, a2, a3


def _dot_f32_by01(a, m01):
    a1, a2, a3 = _split3(a)
    return _dot(a1, m01) + _dot(a2, m01) + _dot(a3, m01)


def _dot_01_by_f32(m01, a):
    a1, a2, a3 = _split3(a)
    return _dot(m01, a1) + _dot(m01, a2) + _dot(m01, a3)


def _layer_norm(x, g, b):
    mu = jnp.mean(x, axis=-1, keepdims=True)
    xc = x - mu
    var = jnp.mean(xc * xc, axis=-1, keepdims=True)
    return xc * lax.rsqrt(var + LN_EPS) * g + b


def _silu(x):
    return x * jax.nn.sigmoid(x)


def _softplus(x):
    return jnp.maximum(x, 0.0) + jnp.log(1.0 + jnp.exp(-jnp.abs(x)))


def _inproj_kernel(x_ref, wz_ref, wx_ref, wd_ref, z_ref, xbc_ref, dt_ref):
    xb = x_ref[...].astype(BF16)
    z_ref[...] = _dot(xb, wz_ref[...]).astype(z_ref.dtype)
    xbc_ref[...] = _dot(xb, wx_ref[...]).astype(xbc_ref.dtype)
    dt_ref[...] = _dot(xb, wd_ref[...])


def _inproj(xf, wz, wx, wd, tm):
    n, d = xf.shape
    d_inner, conv_dim = wz.shape[1], wx.shape[1]
    return pl.pallas_call(
        _inproj_kernel,
        grid=(n // tm,),
        in_specs=[pl.BlockSpec((tm, d), lambda i: (i, 0)),
                  _const_spec(wz.shape), _const_spec(wx.shape), _const_spec(wd.shape)],
        out_specs=[pl.BlockSpec((tm, d_inner), lambda i: (i, 0)),
                   pl.BlockSpec((tm, conv_dim), lambda i: (i, 0)),
                   pl.BlockSpec((tm, LANES), lambda i: (i, 0))],
        out_shape=[jax.ShapeDtypeStruct((n, d_inner), BF16),
                   jax.ShapeDtypeStruct((n, conv_dim), BF16),
                   jax.ShapeDtypeStruct((n, LANES), F32)],
        compiler_params=pltpu.CompilerParams(dimension_semantics=("parallel",),
                                             vmem_limit_bytes=VMEM_LIMIT),
        name="inproj",
    )(xf, wz, wx, wd)


def _ssd_kernel(xbc_ref, z_ref, dtr_ref, cprev_ref, hprev_ref, cw_ref, cb_ref, dtb_ref, alog_ref,
                dskip_ref, ng_ref, expand_ref, expand_t_ref,
                y_ref, cnew_ref, hlast_ref,
                h_s, xwin_s, xc_s, dte_s, ece_s, wee_s,
                *, chunk, t_valid, n_groups, heads_per_group, head_dim, d_state):
    L = chunk
    G, R, P, N = n_groups, heads_per_group, head_dim, d_state
    GW = R * P
    d_inner = G * GW
    c = pl.program_id(1)
    nc = pl.num_programs(1)

    @pl.when(c == 0)
    def _():
        h_s[...] = hprev_ref[...]
        xwin_s[XWIN_PAD - (CONV_W - 1):XWIN_PAD, :] = cprev_ref[...]

    xwin_s[XWIN_PAD:XWIN_PAD + L, :] = xbc_ref[...].astype(F32)
    acc = xwin_s[XWIN_PAD:XWIN_PAD + L, :] * cw_ref[CONV_W - 1:CONV_W, :]
    for kk in range(CONV_W - 1):
        off = XWIN_PAD - (CONV_W - 1) + kk
        acc = acc + xwin_s[off:off + L, :] * cw_ref[kk:kk + 1, :]
    xc_s[...] = _silu(acc + cb_ref[...])

    last_row = (t_valid - 1) % L + 1

    @pl.when(c == nc - 1)
    def _():
        cnew_ref[...] = xwin_s[XWIN_PAD + last_row - (CONV_W - 1):XWIN_PAD + last_row, :]

    @pl.when(c < nc - 1)
    def _():
        xwin_s[XWIN_PAD - (CONV_W - 1):XWIN_PAD, :] = xwin_s[XWIN_PAD + L - (CONV_W - 1):XWIN_PAD + L, :]

    dt = _softplus(dtr_ref[...] + dtb_ref[...])
    if t_valid % L:
        row = lax.broadcasted_iota(jnp.int32, (L, LANES), 0) + c * L
        dt = jnp.where(row < t_valid, dt, 0.0)
    a_neg = -jnp.exp(alog_ref[...])
    ri = lax.broadcasted_iota(jnp.int32, (L, L), 0)
    ci = lax.broadcasted_iota(jnp.int32, (L, L), 1)
    causal = ci <= ri
    tri = causal.astype(BF16)
    cum = _dot_01_by_f32(tri, dt * a_neg)
    cum_t = cum.T
    cum_last = cum[L - 1:L, :]
    expand = expand_ref[...]
    dte_s[...] = _dot_f32_by01(dt, expand)
    ece_s[...] = _dot_f32_by01(jnp.exp(cum), expand)
    wee_s[...] = _dot_f32_by01(jnp.exp(cum_last - cum) * dt, expand)
    cum_last_cols = jnp.broadcast_to(cum_t[:, L - 1:L], (LANES, N))
    dec_rows = jnp.exp(_dot_01_by_f32(expand_t_ref[...], cum_last_cols))

    lane = lax.broadcasted_iota(jnp.int32, (L, 2 * P), 1)
    lo_half = lane < P

    for g in range(G):
        gs = slice(g * GW, (g + 1) * GW)
        b_g = xc_s[:, d_inner + g * N:d_inner + (g + 1) * N].astype(BF16)
        c_g = xc_s[:, d_inner + G * N + g * N:d_inner + G * N + (g + 1) * N].astype(BF16)
        x_g = xc_s[:, gs]
        cb = _dot_nt(c_g, b_g)
        h_g = h_s[gs, :]
        y_off = _dot_nt(c_g, h_g.astype(BF16)) * ece_s[:, gs]
        xdt = (x_g * dte_s[:, gs]).astype(BF16)
        y_parts = []
        for pr in range(R // 2):
            ms = []
            for r in (2 * pr, 2 * pr + 1):
                hh = g * R + r
                seg = cum[:, hh:hh + 1] - cum_t[hh:hh + 1, :]
                dec = jnp.where(causal, jnp.exp(jnp.minimum(seg, 0.0)), 0.0)
                ms.append((cb * dec).astype(BF16))
            m2 = jnp.concatenate(ms, axis=1)
            xp = xdt[:, pr * 2 * P:(pr + 1) * 2 * P]
            zero = jnp.zeros_like(xp)
            w2 = jnp.concatenate([jnp.where(lo_half, xp, zero), jnp.where(lo_half, zero, xp)], axis=0)
            y_parts.append(_dot(m2, w2))
        y = jnp.concatenate(y_parts, axis=1) + y_off + dskip_ref[:, gs] * x_g
        zg = z_ref[:, gs].astype(F32)
        y = y * _silu(zg)
        y = y * lax.rsqrt(jnp.mean(y * y, axis=-1, keepdims=True) + RMS_EPS)
        y_ref[:, gs] = (y * ng_ref[:, gs]).astype(y_ref.dtype)
        xw_t = (x_g * wee_s[:, gs]).T.astype(BF16)
        h_s[gs, :] = h_g * dec_rows[gs, :] + _dot(xw_t, b_g)

    @pl.when(c == nc - 1)
    def _():
        hlast_ref[...] = h_s[...]


def _ssd(xbc, z, dtr, conv_prev, h_prev, cw, cb, dtb, alog, dskip_e, ng, expand, expand_t, t_valid):
    bsz, t_pad, conv_dim = xbc.shape
    d_inner = z.shape[2]
    L = SSD_CHUNK
    nc = t_pad // L
    n_heads = d_inner // SSM_HEAD_DIM
    kern = functools.partial(_ssd_kernel, chunk=L, t_valid=t_valid, n_groups=SSM_GROUPS,
                             heads_per_group=n_heads // SSM_GROUPS, head_dim=SSM_HEAD_DIM, d_state=D_STATE)
    tok = lambda w: pl.BlockSpec((None, L, w), lambda b, c: (b, c, 0))
    per_b = lambda s: pl.BlockSpec((None,) + s, lambda b, c: (b,) + (0,) * len(s))
    return pl.pallas_call(
        kern,
        grid=(bsz, nc),
        in_specs=[tok(conv_dim), tok(d_inner), tok(LANES),
                  per_b((CONV_W - 1, conv_dim)), per_b((d_inner, D_STATE)),
                  _const_spec(cw.shape), _const_spec(cb.shape), _const_spec(dtb.shape), _const_spec(alog.shape),
                  _const_spec(dskip_e.shape), _const_spec(ng.shape), _const_spec(expand.shape),
                  _const_spec(expand_t.shape)],
        out_specs=[tok(d_inner), per_b((CONV_W - 1, conv_dim)), per_b((d_inner, D_STATE))],
        out_shape=[jax.ShapeDtypeStruct((bsz, t_pad, d_inner), BF16),
                   jax.ShapeDtypeStruct((bsz, CONV_W - 1, conv_dim), F32),
                   jax.ShapeDtypeStruct((bsz, d_inner, D_STATE), F32)],
        scratch_shapes=[pltpu.VMEM((d_inner, D_STATE), F32),
                        pltpu.VMEM((XWIN_PAD + L, conv_dim), F32),
                        pltpu.VMEM((L, conv_dim), F32),
                        pltpu.VMEM((L, d_inner), F32),
                        pltpu.VMEM((L, d_inner), F32),
                        pltpu.VMEM((L, d_inner), F32)],
        compiler_params=pltpu.CompilerParams(dimension_semantics=("parallel", "arbitrary"),
                                             vmem_limit_bytes=VMEM_LIMIT),
        name="ssd",
    )(xbc, z, dtr, conv_prev, h_prev, cw, cb, dtb, alog, dskip_e, ng, expand, expand_t)


def _post_kernel(*refs, ff_chunk, with_kv):
    if with_kv:
        (x_ref, m_ref, p_ref, wo_ref, g1_ref, b1_ref, w1_ref, w2_ref, g2_ref, b2_ref, wg_ref, wp_ref,
         kg_ref, kb_ref, wkv_ref, o_ref, k_ref, v_ref, k16_ref, v16_ref) = refs
    else:
        (x_ref, m_ref, p_ref, wo_ref, g1_ref, b1_ref, w1_ref, w2_ref, g2_ref, b2_ref, wg_ref, wp_ref,
         o_ref) = refs
    x = x_ref[...]
    mix = _dot(m_ref[...], wo_ref[...])
    x1 = _layer_norm(ALPHA * x + mix, g1_ref[...], b1_ref[...])
    x1b = x1.astype(BF16)
    d_ff = w1_ref.shape[1]
    acc = jnp.zeros_like(x)
    for cc in range(d_ff // ff_chunk):
        cs = slice(cc * ff_chunk, (cc + 1) * ff_chunk)
        h = jnp.square(jnp.maximum(_dot(x1b, w1_ref[:, cs]), 0.0)).astype(BF16)
        acc = acc + _dot(h, w2_ref[cs, :])
    x2 = _layer_norm(ALPHA * x1 + acc, g2_ref[...], b2_ref[...])
    gate = jax.nn.sigmoid(_dot(x2.astype(BF16), wg_ref[...]))
    pe = _dot(p_ref[...].astype(BF16), wp_ref[...])
    x3 = x2 + gate * pe
    o_ref[...] = x3
    if with_kv:
        kvn = _layer_norm(x3, kg_ref[...], kb_ref[...]).astype(BF16)
        kv = _dot(kvn, wkv_ref[...])
        w = k_ref.shape[1]
        k_ref[...] = kv[:, :w]
        v_ref[...] = kv[:, w:]
        k16_ref[...] = kv[:, :w].astype(BF16)
        v16_ref[...] = kv[:, w:].astype(BF16)


def _post(xf, mf, pf, wo, g1, b1, w1, w2, g2, b2, wg, wp, kv_params, tm):
    n, d = xf.shape
    km, pd = mf.shape[1], pf.shape[1]
    with_kv = kv_params is not None
    row = lambda w: pl.BlockSpec((tm, w), lambda i: (i, 0))
    ins = [xf, mf, pf, wo, g1, b1, w1, w2, g2, b2, wg, wp]
    in_specs = [row(d), row(km), row(pd)] + [_const_spec(a.shape) for a in ins[3:]]
    out_specs = [row(d)]
    out_shape = [jax.ShapeDtypeStruct((n, d), F32)]
    if with_kv:
        ins += list(kv_params)
        in_specs += [_const_spec(a.shape) for a in kv_params]
        w = kv_params[2].shape[1] // 2
        out_specs += [row(w)] * 4
        out_shape += [jax.ShapeDtypeStruct((n, w), F32)] * 2 + [jax.ShapeDtypeStruct((n, w), BF16)] * 2
    return pl.pallas_call(
        functools.partial(_post_kernel, ff_chunk=min(1024, w1.shape[1]), with_kv=with_kv),
        grid=(n // tm,),
        in_specs=in_specs, out_specs=out_specs, out_shape=out_shape,
        compiler_params=pltpu.CompilerParams(dimension_semantics=("parallel",),
                                             vmem_limit_bytes=VMEM_LIMIT),
        name="post_kv" if with_kv else "post",
    )(*ins)


def _attn_kernel(x_ref, wq_ref, k_ref, v_ref, ut_ref, o_ref, q_s, suf_s, acc_s,
                 *, tq, q_offset, n_heads, head_dim, scale):
    KB = SB_BLOCK
    qi = pl.program_id(1)
    q_s[...] = _dot(x_ref[...].astype(BF16), wq_ref[...]).astype(BF16)
    suf_s[...] = jnp.zeros_like(suf_s)
    acc_s[...] = jnp.zeros_like(acc_s)
    q_min = q_offset + qi * tq
    n_full = q_min // KB
    n_diag = -(-tq // KB)
    ut = ut_ref[...]

    def key_block(kstart, masked):
        if masked:
            qpos = q_min + lax.broadcasted_iota(jnp.int32, (tq, KB), 0)
            kpos = kstart + lax.broadcasted_iota(jnp.int32, (tq, KB), 1)
            mask = kpos < qpos
        for h in range(n_heads):
            hs = slice(h * head_dim, (h + 1) * head_dim)
            z = _dot_nt(q_s[:, hs], k_ref[pl.ds(kstart, KB), hs]) * scale
            lk = -_softplus(z)
            lsig = z + lk
            if masked:
                lk = jnp.where(mask, lk, 0.0)
            it = _dot(lk.astype(BF16), ut)
            w = jnp.exp(lsig + it[:, :KB] + suf_s[h])
            if masked:
                w = jnp.where(mask, w, 0.0)
            acc_s[:, hs] += _dot(w.astype(BF16), v_ref[pl.ds(kstart, KB), hs])
            suf_s[h] += it[:, KB:]

    for d in reversed(range(n_diag)):
        key_block(pl.multiple_of(q_min + d * KB, KB), True)

    def body(i, carry):
        key_block(pl.multiple_of((n_full - 1 - i) * KB, KB), False)
        return carry

    lax.fori_loop(0, n_full, body, 0)
    o_ref[...] = acc_s[...].astype(o_ref.dtype)


def _attn(x, wq, k16, v16, ut, q_offset, tq):
    bsz, t, d = x.shape
    tk, width = k16.shape[1], k16.shape[2]
    n_heads = width // SB_HEAD_DIM
    assert q_offset % SB_BLOCK == 0 and t % tq == 0 and (tq % SB_BLOCK == 0 or t == tq)
    assert tk % SB_BLOCK == 0 and tk >= q_offset + (-(-t // SB_BLOCK)) * SB_BLOCK
    kern = functools.partial(_attn_kernel, tq=tq, q_offset=q_offset, n_heads=n_heads,
                             head_dim=SB_HEAD_DIM, scale=SB_HEAD_DIM ** -0.5)
    return pl.pallas_call(
        kern,
        grid=(bsz, t // tq),
        in_specs=[pl.BlockSpec((None, tq, d), lambda b, i: (b, i, 0)),
                  _const_spec(wq.shape),
                  pl.BlockSpec((None, tk, width), lambda b, i: (b, 0, 0)),
                  pl.BlockSpec((None, tk, width), lambda b, i: (b, 0, 0)),
                  _const_spec(ut.shape)],
        out_specs=pl.BlockSpec((None, tq, width), lambda b, i: (b, i, 0)),
        out_shape=jax.ShapeDtypeStruct((bsz, t, width), BF16),
        scratch_shapes=[pltpu.VMEM((tq, width), BF16),
                        pltpu.VMEM((n_heads, tq, SB_BLOCK), F32),
                        pltpu.VMEM((tq, width), F32)],
        compiler_params=pltpu.CompilerParams(dimension_semantics=("parallel", "arbitrary"),
                                             vmem_limit_bytes=VMEM_LIMIT),
        name="sb_attn",
    )(x, wq, k16, v16, ut)


def _row(v):
    return v.reshape(1, -1).astype(F32)


def _pad_lanes(v):
    return jnp.pad(v.reshape(1, -1).astype(F32), ((0, 0), (0, LANES - v.size)))


def _prep_weights(prm):
    d_inner = prm['a_w_out'].shape[1]
    conv_dim = prm['a_conv_w'].shape[2]
    n_heads = prm['a_dt_bias'].shape[1]
    w = dict(prm)
    w['wz'] = prm['a_w_in'][:, :, :d_inner].astype(BF16)
    w['wx'] = prm['a_w_in'][:, :, d_inner:d_inner + conv_dim].astype(BF16)
    w['wd'] = jnp.pad(prm['a_w_in'][:, :, d_inner + conv_dim:], ((0, 0), (0, 0), (0, LANES - n_heads))).astype(BF16)
    for name in ('a_w_out', 'w_kv', 'b_w_q', 'b_w_out', 'mlp_w1', 'mlp_w2', 'ple_w', 'ple_gate_w'):
        w[name] = prm[name].astype(BF16)
    head_of_channel = jnp.arange(d_inner) // SSM_HEAD_DIM
    w['expand'] = (jnp.arange(LANES)[:, None] == head_of_channel[None, :]).astype(BF16)
    idx = jnp.arange(SB_BLOCK)
    upper = (idx[:, None] > idx[None, :]).astype(BF16)
    w['ut'] = jnp.concatenate([upper, jnp.ones((SB_BLOCK, SB_BLOCK), BF16)], axis=1)
    return w


def _run_trunk(x, p, conv_prev, ssm_prev, k_past, v_past, q_offset, w, tm, tq):
    bsz, t, d = x.shape
    n = bsz * t
    d_inner = w['a_w_out'].shape[1]
    xf = x.reshape(n, d)
    conv_states, ssm_states = [], []
    k_new = v_new = k16 = v16 = None
    t_pad = -(-t // SSD_CHUNK) * SSD_CHUNK
    for i in range(DEPTH):
        if i < N_A:
            z, xbc, dtr = _inproj(xf, w['wz'][i], w['wx'][i], w['wd'][i], tm)
            z, xbc, dtr = (a.reshape(bsz, t, -1) for a in (z, xbc, dtr))
            if t_pad != t:
                z, xbc, dtr = (jnp.pad(a, ((0, 0), (0, t_pad - t), (0, 0))) for a in (z, xbc, dtr))
            y, cnew, hlast = _ssd(
                xbc, z, dtr, conv_prev[i], ssm_prev[i].reshape(bsz, d_inner, D_STATE),
                w['a_conv_w'][i], _row(w['a_conv_b'][i]), _pad_lanes(w['a_dt_bias'][i]),
                _pad_lanes(w['a_A_log'][i]), _row(jnp.repeat(w['a_d_skip'][i], SSM_HEAD_DIM)),
                _row(w['a_norm_g'][i]), w['expand'], w['expand'].T, t)
            conv_states.append(cnew)
            ssm_states.append(hlast.reshape(bsz, d_inner // SSM_HEAD_DIM, SSM_HEAD_DIM, D_STATE))
            mf = y[:, :t].reshape(n, d_inner)
            wo = w['a_w_out'][i]
        else:
            j = i - N_A
            o = _attn(xf.reshape(bsz, t, d), w['b_w_q'][j], k16, v16, w['ut'], q_offset, tq)
            mf = o.reshape(n, -1)
            wo = w['b_w_out'][j]
        kv_params = (_row(w['kv_norm_g']), _row(w['kv_norm_b']), w['w_kv']) if i == N_A - 1 else None
        outs = _post(xf, mf, p[i].reshape(n, -1), wo, _row(w['ln1_g'][i]), _row(w['ln1_b'][i]),
                     w['mlp_w1'][i], w['mlp_w2'][i], _row(w['ln2_g'][i]), _row(w['ln2_b'][i]),
                     w['ple_gate_w'][i], w['ple_w'][i], kv_params, tm)
        xf = outs[0]
        if kv_params is not None:
            k_new, v_new, kb, vb = outs[1:]
            width = kb.shape[1]
            k16, v16 = kb.reshape(bsz, t, width), vb.reshape(bsz, t, width)
            if k_past is not None:
                k16 = jnp.concatenate([k_past.reshape(bsz, -1, width).astype(BF16), k16], axis=1)
                v16 = jnp.concatenate([v_past.reshape(bsz, -1, width).astype(BF16), v16], axis=1)
            tk = k16.shape[1]
            tk_pad = -(-tk // SB_BLOCK) * SB_BLOCK
            if tk_pad != tk:
                k16 = jnp.pad(k16, ((0, 0), (0, tk_pad - tk), (0, 0)))
                v16 = jnp.pad(v16, ((0, 0), (0, tk_pad - tk), (0, 0)))
    n_sb = k_new.shape[1] // SB_HEAD_DIM
    return (xf.reshape(bsz, t, d), jnp.stack(ssm_states), jnp.stack(conv_states),
            k_new.reshape(bsz, t, n_sb, SB_HEAD_DIM), v_new.reshape(bsz, t, n_sb, SB_HEAD_DIM))


def kernel(x_prompt, x_sample, p_prompt, p_sample, state_ssm, state_conv, cache_k, cache_v, a_w_in, a_conv_w, a_conv_b, a_dt_bias, a_A_log, a_d_skip, a_norm_g, a_w_out, kv_norm_g, kv_norm_b, w_kv, b_w_q, b_w_out, ln1_g, ln1_b, ln2_g, ln2_b, mlp_w1, mlp_w2, ple_w, ple_gate_w):
    prm = {
        'a_w_in': a_w_in, 'a_conv_w': a_conv_w, 'a_conv_b': a_conv_b, 'a_dt_bias': a_dt_bias,
        'a_A_log': a_A_log, 'a_d_skip': a_d_skip, 'a_norm_g': a_norm_g, 'a_w_out': a_w_out,
        'kv_norm_g': kv_norm_g, 'kv_norm_b': kv_norm_b, 'w_kv': w_kv,
        'b_w_q': b_w_q, 'b_w_out': b_w_out,
        'ln1_g': ln1_g, 'ln1_b': ln1_b, 'ln2_g': ln2_g, 'ln2_b': ln2_b,
        'mlp_w1': mlp_w1, 'mlp_w2': mlp_w2, 'ple_w': ple_w, 'ple_gate_w': ple_gate_w,
    }
    w = _prep_weights(prm)
    bp, tp, _ = x_prompt.shape
    bs, ts, _ = x_sample.shape
    n_heads = a_dt_bias.shape[1]
    conv0 = jnp.zeros((N_A, bp, CONV_W - 1, a_conv_w.shape[2]), F32)
    ssm0 = jnp.zeros((N_A, bp, n_heads, SSM_HEAD_DIM, D_STATE), F32)
    y_p, ssm_p, conv_p, k_p, v_p = _run_trunk(x_prompt, p_prompt, conv0, ssm0, None, None, 0, w,
                                              tm=min(512, bp * tp), tq=min(256, tp))
    y_s, ssm_s, conv_s, k_s, v_s = _run_trunk(x_sample, p_sample, state_conv, state_ssm, cache_k, cache_v,
                                              cache_k.shape[1], w, tm=min(512, bs * ts), tq=ts)
    return (y_p, y_s, ssm_p, conv_p, k_p, v_p, ssm_s, conv_s, k_s, v_s)
```

```python
import functools

import jax
import jax.numpy as jnp
from jax import lax
from jax.experimental import pallas as pl
from jax.experimental.pallas import tpu as pltpu

F32 = jnp.float32
BF16 = jnp.bfloat16

DEPTH = 4
N_A = DEPTH // 2
SSM_HEAD_DIM = 64
SSM_GROUPS = 8
D_STATE = 128
CONV_W = 4
SB_HEAD_DIM = 128
SB_BLOCK = 128
ALPHA = (2.0 * DEPTH) ** 0.25
LN_EPS = 1e-5
RMS_EPS = 1e-5

LANES = 128
SSD_CHUNK = 128
XWIN_PAD = 8
VMEM_LIMIT = 56 * 1024 * 1024
LOG2E = 1.4426950408889634
SIGN_BIT = -2147483648
UNDERFLOW_LOG2 = 152.0


def _const_spec(shape):
    nd = len(shape)
    return pl.BlockSpec(shape, lambda *_: (0,) * nd, pipeline_mode=pl.Buffered(1))


def _dot(a, b):
    return jnp.dot(a, b, preferred_element_type=F32)


def _dot_nt(a, b):
    return lax.dot_general(a, b, (((1,), (1,)), ((), ())), preferred_element_type=F32)


def _split3(a):
    a1 = a.astype(BF16)
    r1 = a - a1.astype(F32)
    a2 = r1.astype(BF16)
    a3 = (r1 - a2.astype(F32)).astype(BF16)
    return a1, a2, a3


def _dot_f32_by01(a, m01):
    a1, a2, a3 = _split3(a)
    return _dot(a1, m01) + _dot(a2, m01) + _dot(a3, m01)


def _dot_01_by_f32(m01, a):
    a1, a2, a3 = _split3(a)
    return _dot(m01, a1) + _dot(m01, a2) + _dot(m01, a3)


def _layer_norm(x, g, b):
    mu = jnp.mean(x, axis=-1, keepdims=True)
    xc = x - mu
    var = jnp.mean(xc * xc, axis=-1, keepdims=True)
    return xc * lax.rsqrt(var + LN_EPS) * g + b


def _silu(x):
    return x * jax.nn.sigmoid(x)


def _softplus(x):
    return jnp.maximum(x, 0.0) + jnp.log(1.0 + jnp.exp(-jnp.abs(x)))


def _inproj_kernel(x_ref, wz_ref, wx_ref, wd_ref, z_ref, xbc_ref, dt_ref):
    xb = x_ref[...].astype(BF16)
    z_ref[...] = _dot(xb, wz_ref[...]).astype(z_ref.dtype)
    xbc_ref[...] = _dot(xb, wx_ref[...]).astype(xbc_ref.dtype)
    dt_ref[...] = _dot(xb, wd_ref[...])


def _inproj(xf, wz, wx, wd, tm):
    n, d = xf.shape
    d_inner, conv_dim = wz.shape[1], wx.shape[1]
    return pl.pallas_call(
        _inproj_kernel,
        grid=(n // tm,),
        in_specs=[pl.BlockSpec((tm, d), lambda i: (i, 0)),
                  _const_spec(wz.shape), _const_spec(wx.shape), _const_spec(wd.shape)],
        out_specs=[pl.BlockSpec((tm, d_inner), lambda i: (i, 0)),
                   pl.BlockSpec((tm, conv_dim), lambda i: (i, 0)),
                   pl.BlockSpec((tm, LANES), lambda i: (i, 0))],
        out_shape=[jax.ShapeDtypeStruct((n, d_inner), BF16),
                   jax.ShapeDtypeStruct((n, conv_dim), BF16),
                   jax.ShapeDtypeStruct((n, LANES), F32)],
        compiler_params=pltpu.CompilerParams(dimension_semantics=("parallel",),
                                             vmem_limit_bytes=VMEM_LIMIT),
        name="inproj",
    )(xf, wz, wx, wd)


def _ssd_kernel(xbc_ref, z_ref, dtr_ref, cprev_ref, hprev_ref, cw_ref, cb_ref, dtb_ref, alog_ref,
                dskip_ref, ng_ref, expand_ref, expand_t_ref,
                y_ref, cnew_ref, hlast_ref,
                h_s, xwin_s, xc_s, dte_s, ece_s, wee_s,
                *, chunk, t_valid, n_groups, heads_per_group, head_dim, d_state):
    L = chunk
    G, R, P, N = n_groups, heads_per_group, head_dim, d_state
    GW = R * P
    d_inner = G * GW
    c = pl.program_id(1)
    nc = pl.num_programs(1)

    @pl.when(c == 0)
    def _():
        h_s[...] = hprev_ref[...]
        xwin_s[XWIN_PAD - (CONV_W - 1):XWIN_PAD, :] = cprev_ref[...]

    xwin_s[XWIN_PAD:XWIN_PAD + L, :] = xbc_ref[...].astype(F32)
    acc = xwin_s[XWIN_PAD:XWIN_PAD + L, :] * cw_ref[CONV_W - 1:CONV_W, :]
    for kk in range(CONV_W - 1):
        off = XWIN_PAD - (CONV_W - 1) + kk
        acc = acc + xwin_s[off:off + L, :] * cw_ref[kk:kk + 1, :]
    xc_s[...] = _silu(acc + cb_ref[...])

    last_row = (t_valid - 1) % L + 1

    @pl.when(c == nc - 1)
    def _():
        cnew_ref[...] = xwin_s[XWIN_PAD + last_row - (CONV_W - 1):XWIN_PAD + last_row, :]

    @pl.when(c < nc - 1)
    def _():
        xwin_s[XWIN_PAD - (CONV_W - 1):XWIN_PAD, :] = xwin_s[XWIN_PAD + L - (CONV_W - 1):XWIN_PAD + L, :]

    dt = _softplus(dtr_ref[...] + dtb_ref[...])
    if t_valid % L:
        row = lax.broadcasted_iota(jnp.int32, (L, LANES), 0) + c * L
        dt = jnp.where(row < t_valid, dt, 0.0)
    a_neg = -jnp.exp(alog_ref[...])
    ri = lax.broadcasted_iota(jnp.int32, (L, L), 0)
    ci = lax.broadcasted_iota(jnp.int32, (L, L), 1)
    causal = ci <= ri
    tri = causal.astype(BF16)
    cum = _dot_01_by_f32(tri, dt * a_neg)
    cum_t = cum.T
    cum_last = cum[L - 1:L, :]
    expand = expand_ref[...]
    dte_s[...] = _dot_f32_by01(dt, expand)
    ece_s[...] = _dot_f32_by01(jnp.exp(cum), expand)
    wee_s[...] = _dot_f32_by01(jnp.exp(cum_last - cum) * dt, expand)
    cum_last_cols = jnp.broadcast_to(cum_t[:, L - 1:L], (LANES, N))
    dec_rows = jnp.exp(_dot_01_by_f32(expand_t_ref[...], cum_last_cols))

    lane = lax.broadcasted_iota(jnp.int32, (L, 2 * P), 1)
    lo_half = lane < P

    for g in range(G):
        gs = slice(g * GW, (g + 1) * GW)
        b_g = xc_s[:, d_inner + g * N:d_inner + (g + 1) * N].astype(BF16)
        c_g = xc_s[:, d_inner + G * N + g * N:d_inner + G * N + (g + 1) * N].astype(BF16)
        x_g = xc_s[:, gs]
        cb = _dot_nt(c_g, b_g)
        h_g = h_s[gs, :]
        y_off = _dot_nt(c_g, h_g.astype(BF16)) * ece_s[:, gs]
        xdt = (x_g * dte_s[:, gs]).astype(BF16)
        y_parts = []
        for pr in range(R // 2):
            ms = []
            for r in (2 * pr, 2 * pr + 1):
                hh = g * R + r
                seg = cum[:, hh:hh + 1] - cum_t[hh:hh + 1, :]
                dec = jnp.where(causal, jnp.exp(jnp.minimum(seg, 0.0)), 0.0)
                ms.append((cb * dec).astype(BF16))
            m2 = jnp.concatenate(ms, axis=1)
            xp = xdt[:, pr * 2 * P:(pr + 1) * 2 * P]
            zero = jnp.zeros_like(xp)
            w2 = jnp.concatenate([jnp.where(lo_half, xp, zero), jnp.where(lo_half, zero, xp)], axis=0)
            y_parts.append(_dot(m2, w2))
        y = jnp.concatenate(y_parts, axis=1) + y_off + dskip_ref[:, gs] * x_g
        zg = z_ref[:, gs].astype(F32)
        y = y * _silu(zg)
        y = y * lax.rsqrt(jnp.mean(y * y, axis=-1, keepdims=True) + RMS_EPS)
        y_ref[:, gs] = (y * ng_ref[:, gs]).astype(y_ref.dtype)
        xw_t = (x_g * wee_s[:, gs]).T.astype(BF16)
        h_s[gs, :] = h_g * dec_rows[gs, :] + _dot(xw_t, b_g)

    @pl.when(c == nc - 1)
    def _():
        hlast_ref[...] = h_s[...]


def _ssd(xbc, z, dtr, conv_prev, h_prev, cw, cb, dtb, alog, dskip_e, ng, expand, expand_t, t_valid):
    bsz, t_pad, conv_dim = xbc.shape
    d_inner = z.shape[2]
    L = SSD_CHUNK
    nc = t_pad // L
    n_heads = d_inner // SSM_HEAD_DIM
    kern = functools.partial(_ssd_kernel, chunk=L, t_valid=t_valid, n_groups=SSM_GROUPS,
                             heads_per_group=n_heads // SSM_GROUPS, head_dim=SSM_HEAD_DIM, d_state=D_STATE)
    tok = lambda w: pl.BlockSpec((None, L, w), lambda b, c: (b, c, 0))
    per_b = lambda s: pl.BlockSpec((None,) + s, lambda b, c: (b,) + (0,) * len(s))
    return pl.pallas_call(
        kern,
        grid=(bsz, nc),
        in_specs=[tok(conv_dim), tok(d_inner), tok(LANES),
                  per_b((CONV_W - 1, conv_dim)), per_b((d_inner, D_STATE)),
                  _const_spec(cw.shape), _const_spec(cb.shape), _const_spec(dtb.shape), _const_spec(alog.shape),
                  _const_spec(dskip_e.shape), _const_spec(ng.shape), _const_spec(expand.shape),
                  _const_spec(expand_t.shape)],
        out_specs=[tok(d_inner), per_b((CONV_W - 1, conv_dim)), per_b((d_inner, D_STATE))],
        out_shape=[jax.ShapeDtypeStruct((bsz, t_pad, d_inner), BF16),
                   jax.ShapeDtypeStruct((bsz, CONV_W - 1, conv_dim), F32),
                   jax.ShapeDtypeStruct((bsz, d_inner, D_STATE), F32)],
        scratch_shapes=[pltpu.VMEM((d_inner, D_STATE), F32),
                        pltpu.VMEM((XWIN_PAD + L, conv_dim), F32),
                        pltpu.VMEM((L, conv_dim), F32),
                        pltpu.VMEM((L, d_inner), F32),
                        pltpu.VMEM((L, d_inner), F32),
                        pltpu.VMEM((L, d_inner), F32)],
        compiler_params=pltpu.CompilerParams(dimension_semantics=("parallel", "arbitrary"),
                                             vmem_limit_bytes=VMEM_LIMIT),
        name="ssd",
    )(xbc, z, dtr, conv_prev, h_prev, cw, cb, dtb, alog, dskip_e, ng, expand, expand_t)


def _post_kernel(*refs, ff_chunk, with_kv):
    if with_kv:
        (x_ref, m_ref, p_ref, wo_ref, g1_ref, b1_ref, w1_ref, w2_ref, g2_ref, b2_ref, wg_ref, wp_ref,
         kg_ref, kb_ref, wkv_ref, o_ref, k_ref, v_ref, k16_ref, v16_ref) = refs
    else:
        (x_ref, m_ref, p_ref, wo_ref, g1_ref, b1_ref, w1_ref, w2_ref, g2_ref, b2_ref, wg_ref, wp_ref,
         o_ref) = refs
    x = x_ref[...]
    mix = _dot(m_ref[...], wo_ref[...])
    x1 = _layer_norm(ALPHA * x + mix, g1_ref[...], b1_ref[...])
    x1b = x1.astype(BF16)
    d_ff = w1_ref.shape[1]
    acc = jnp.zeros_like(x)
    for cc in range(d_ff // ff_chunk):
        cs = slice(cc * ff_chunk, (cc + 1) * ff_chunk)
        h = jnp.square(jnp.maximum(_dot(x1b, w1_ref[:, cs]), 0.0)).astype(BF16)
        acc = acc + _dot(h, w2_ref[cs, :])
    x2 = _layer_norm(ALPHA * x1 + acc, g2_ref[...], b2_ref[...])
    gate = jax.nn.sigmoid(_dot(x2.astype(BF16), wg_ref[...]))
    pe = _dot(p_ref[...].astype(BF16), wp_ref[...])
    x3 = x2 + gate * pe
    o_ref[...] = x3
    if with_kv:
        kvn = _layer_norm(x3, kg_ref[...], kb_ref[...]).astype(BF16)
        kv = _dot(kvn, wkv_ref[...])
        w = k_ref.shape[1]
        k_ref[...] = kv[:, :w]
        v_ref[...] = kv[:, w:]
        k16_ref[...] = kv[:, :w].astype(BF16)
        v16_ref[...] = kv[:, w:].astype(BF16)


def _post(xf, mf, pf, wo, g1, b1, w1, w2, g2, b2, wg, wp, kv_params, tm):
    n, d = xf.shape
    km, pd = mf.shape[1], pf.shape[1]
    with_kv = kv_params is not None
    row = lambda w: pl.BlockSpec((tm, w), lambda i: (i, 0))
    ins = [xf, mf, pf, wo, g1, b1, w1, w2, g2, b2, wg, wp]
    in_specs = [row(d), row(km), row(pd)] + [_const_spec(a.shape) for a in ins[3:]]
    out_specs = [row(d)]
    out_shape = [jax.ShapeDtypeStruct((n, d), F32)]
    if with_kv:
        ins += list(kv_params)
        in_specs += [_const_spec(a.shape) for a in kv_params]
        w = kv_params[2].shape[1] // 2
        out_specs += [row(w)] * 4
        out_shape += [jax.ShapeDtypeStruct((n, w), F32)] * 2 + [jax.ShapeDtypeStruct((n, w), BF16)] * 2
    return pl.pallas_call(
        functools.partial(_post_kernel, ff_chunk=min(1024, w1.shape[1]), with_kv=with_kv),
        grid=(n // tm,),
        in_specs=in_specs, out_specs=out_specs, out_shape=out_shape,
        compiler_params=pltpu.CompilerParams(dimension_semantics=("parallel",),
                                             vmem_limit_bytes=VMEM_LIMIT),
        name="post_kv" if with_kv else "post",
    )(*ins)


def _attn_kernel(x_ref, wq_ref, k_ref, v_ref, ut_ref, o_ref, q_s, suf_s, acc_s,
                 *, tq, q_offset, n_heads, head_dim, scale):
    KB = SB_BLOCK
    qi = pl.program_id(1)
    q = _dot(x_ref[...].astype(BF16), wq_ref[...]) * (scale * LOG2E)
    q_s[...] = q.astype(BF16)
    suf_s[...] = jnp.zeros_like(suf_s)
    acc_s[...] = jnp.zeros_like(acc_s)
    q_min = q_offset + qi * tq
    n_full = q_min // KB
    n_diag = -(-tq // KB)
    ut = ut_ref[...]
    heads = [slice(h * head_dim, (h + 1) * head_dim) for h in range(n_heads)]

    def key_block(kstart, masked):
        if masked:
            qpos = q_min + lax.broadcasted_iota(jnp.int32, (tq, KB), 0)
            kpos = kstart + lax.broadcasted_iota(jnp.int32, (tq, KB), 1)
            mask = kpos < qpos
        z2 = [_dot_nt(q_s[:, hs], k_ref[pl.ds(kstart, KB), hs]) for hs in heads]
        sp2, lsig2 = [], []
        for z in z2:
            neg_abs = lax.bitcast_convert_type(lax.bitcast_convert_type(z, jnp.int32) | SIGN_BIT, F32)
            sp = jnp.maximum(z, 0.0) + jnp.log(1.0 + jnp.exp2(neg_abs)) * LOG2E
            lsig2.append(z - sp)
            if masked:
                sp = jnp.where(mask, sp, 0.0)
            sp2.append(sp.astype(BF16))
        it = [_dot(sp, ut) for sp in sp2]
        w = []
        for h in range(n_heads):
            wh = jnp.exp2(lsig2[h] - it[h][:, :KB] - suf_s[h])
            if masked:
                wh = jnp.where(mask, wh, 0.0)
            w.append(wh.astype(BF16))
            suf_s[h] = suf_s[h] + it[h][:, KB:]
        for h, hs in enumerate(heads):
            acc_s[:, hs] = acc_s[:, hs] + _dot(w[h], v_ref[pl.ds(kstart, KB), hs])

    def any_row_alive():
        return (jnp.min(suf_s[...]) < UNDERFLOW_LOG2).astype(jnp.int32)

    for d in reversed(range(n_diag)):
        key_block(pl.multiple_of(q_min + d * KB, KB), True)

    def cond(carry):
        i, alive = carry
        return jnp.logical_and(i < n_full, alive > 0)

    def body(carry):
        i, _ = carry
        key_block(pl.multiple_of((n_full - 1 - i) * KB, KB), False)
        return i + 1, any_row_alive()

    lax.while_loop(cond, body, (jnp.int32(0), any_row_alive()))
    o_ref[...] = acc_s[...].astype(o_ref.dtype)


def _attn(x, wq, k16, v16, ut, q_offset, tq):
    bsz, t, d = x.shape
    tk, width = k16.shape[1], k16.shape[2]
    n_heads = width // SB_HEAD_DIM
    assert q_offset % SB_BLOCK == 0 and t % tq == 0 and (tq % SB_BLOCK == 0 or t == tq)
    assert tk % SB_BLOCK == 0 and tk >= q_offset + (-(-t // SB_BLOCK)) * SB_BLOCK
    kern = functools.partial(_attn_kernel, tq=tq, q_offset=q_offset, n_heads=n_heads,
                             head_dim=SB_HEAD_DIM, scale=SB_HEAD_DIM ** -0.5)
    return pl.pallas_call(
        kern,
        grid=(bsz, t // tq),
        in_specs=[pl.BlockSpec((None, tq, d), lambda b, i: (b, i, 0)),
                  _const_spec(wq.shape),
                  pl.BlockSpec((None, tk, width), lambda b, i: (b, 0, 0)),
                  pl.BlockSpec((None, tk, width), lambda b, i: (b, 0, 0)),
                  _const_spec(ut.shape)],
        out_specs=pl.BlockSpec((None, tq, width), lambda b, i: (b, i, 0)),
        out_shape=jax.ShapeDtypeStruct((bsz, t, width), BF16),
        scratch_shapes=[pltpu.VMEM((tq, width), BF16),
                        pltpu.VMEM((n_heads, tq, SB_BLOCK), F32),
                        pltpu.VMEM((tq, width), F32)],
        compiler_params=pltpu.CompilerParams(dimension_semantics=("parallel", "arbitrary"),
                                             vmem_limit_bytes=VMEM_LIMIT),
        name="sb_attn",
    )(x, wq, k16, v16, ut)


def _row(v):
    return v.reshape(1, -1).astype(F32)


def _pad_lanes(v):
    return jnp.pad(v.reshape(1, -1).astype(F32), ((0, 0), (0, LANES - v.size)))


def _prep_weights(prm):
    d_inner = prm['a_w_out'].shape[1]
    conv_dim = prm['a_conv_w'].shape[2]
    n_heads = prm['a_dt_bias'].shape[1]
    w = dict(prm)
    w['wz'] = prm['a_w_in'][:, :, :d_inner].astype(BF16)
    w['wx'] = prm['a_w_in'][:, :, d_inner:d_inner + conv_dim].astype(BF16)
    w['wd'] = jnp.pad(prm['a_w_in'][:, :, d_inner + conv_dim:], ((0, 0), (0, 0), (0, LANES - n_heads))).astype(BF16)
    for name in ('a_w_out', 'w_kv', 'b_w_q', 'b_w_out', 'mlp_w1', 'mlp_w2', 'ple_w', 'ple_gate_w'):
        w[name] = prm[name].astype(BF16)
    head_of_channel = jnp.arange(d_inner) // SSM_HEAD_DIM
    w['expand'] = (jnp.arange(LANES)[:, None] == head_of_channel[None, :]).astype(BF16)
    idx = jnp.arange(SB_BLOCK)
    upper = (idx[:, None] > idx[None, :]).astype(BF16)
    w['ut'] = jnp.concatenate([upper, jnp.ones((SB_BLOCK, SB_BLOCK), BF16)], axis=1)
    return w


def _run_trunk(x, p, conv_prev, ssm_prev, k_past, v_past, q_offset, w, tm, tq):
    bsz, t, d = x.shape
    n = bsz * t
    d_inner = w['a_w_out'].shape[1]
    xf = x.reshape(n, d)
    conv_states, ssm_states = [], []
    k_new = v_new = k16 = v16 = None
    t_pad = -(-t // SSD_CHUNK) * SSD_CHUNK
    for i in range(DEPTH):
        if i < N_A:
            z, xbc, dtr = _inproj(xf, w['wz'][i], w['wx'][i], w['wd'][i], tm)
            z, xbc, dtr = (a.reshape(bsz, t, -1) for a in (z, xbc, dtr))
            if t_pad != t:
                z, xbc, dtr = (jnp.pad(a, ((0, 0), (0, t_pad - t), (0, 0))) for a in (z, xbc, dtr))
            y, cnew, hlast = _ssd(
                xbc, z, dtr, conv_prev[i], ssm_prev[i].reshape(bsz, d_inner, D_STATE),
                w['a_conv_w'][i], _row(w['a_conv_b'][i]), _pad_lanes(w['a_dt_bias'][i]),
                _pad_lanes(w['a_A_log'][i]), _row(jnp.repeat(w['a_d_skip'][i], SSM_HEAD_DIM)),
                _row(w['a_norm_g'][i]), w['expand'], w['expand'].T, t)
            conv_states.append(cnew)
            ssm_states.append(hlast.reshape(bsz, d_inner // SSM_HEAD_DIM, SSM_HEAD_DIM, D_STATE))
            mf = y[:, :t].reshape(n, d_inner)
            wo = w['a_w_out'][i]
        else:
            j = i - N_A
            o = _attn(xf.reshape(bsz, t, d), w['b_w_q'][j], k16, v16, w['ut'], q_offset, tq)
            mf = o.reshape(n, -1)
            wo = w['b_w_out'][j]
        kv_params = (_row(w['kv_norm_g']), _row(w['kv_norm_b']), w['w_kv']) if i == N_A - 1 else None
        outs = _post(xf, mf, p[i].reshape(n, -1), wo, _row(w['ln1_g'][i]), _row(w['ln1_b'][i]),
                     w['mlp_w1'][i], w['mlp_w2'][i], _row(w['ln2_g'][i]), _row(w['ln2_b'][i]),
                     w['ple_gate_w'][i], w['ple_w'][i], kv_params, tm)
        xf = outs[0]
        if kv_params is not None:
            k_new, v_new, kb, vb = outs[1:]
            width = kb.shape[1]
            k16, v16 = kb.reshape(bsz, t, width), vb.reshape(bsz, t, width)
            if k_past is not None:
                k16 = jnp.concatenate([k_past.reshape(bsz, -1, width).astype(BF16), k16], axis=1)
                v16 = jnp.concatenate([v_past.reshape(bsz, -1, width).astype(BF16), v16], axis=1)
            tk = k16.shape[1]
            tk_pad = -(-tk // SB_BLOCK) * SB_BLOCK
            if tk_pad != tk:
                k16 = jnp.pad(k16, ((0, 0), (0, tk_pad - tk), (0, 0)))
                v16 = jnp.pad(v16, ((0, 0), (0, tk_pad - tk), (0, 0)))
    n_sb = k_new.shape[1] // SB_HEAD_DIM
    return (xf.reshape(bsz, t, d), jnp.stack(ssm_states), jnp.stack(conv_states),
            k_new.reshape(bsz, t, n_sb, SB_HEAD_DIM), v_new.reshape(bsz, t, n_sb, SB_HEAD_DIM))


def kernel(x_prompt, x_sample, p_prompt, p_sample, state_ssm, state_conv, cache_k, cache_v, a_w_in, a_conv_w, a_conv_b, a_dt_bias, a_A_log, a_d_skip, a_norm_g, a_w_out, kv_norm_g, kv_norm_b, w_kv, b_w_q, b_w_out, ln1_g, ln1_b, ln2_g, ln2_b, mlp_w1, mlp_w2, ple_w, ple_gate_w):
    prm = {
        'a_w_in': a_w_in, 'a_conv_w': a_conv_w, 'a_conv_b': a_conv_b, 'a_dt_bias': a_dt_bias,
        'a_A_log': a_A_log, 'a_d_skip': a_d_skip, 'a_norm_g': a_norm_g, 'a_w_out': a_w_out,
        'kv_norm_g': kv_norm_g, 'kv_norm_b': kv_norm_b, 'w_kv': w_kv,
        'b_w_q': b_w_q, 'b_w_out': b_w_out,
        'ln1_g': ln1_g, 'ln1_b': ln1_b, 'ln2_g': ln2_g, 'ln2_b': ln2_b,
        'mlp_w1': mlp_w1, 'mlp_w2': mlp_w2, 'ple_w': ple_w, 'ple_gate_w': ple_gate_w,
    }
    w = _prep_weights(prm)
    bp, tp, _ = x_prompt.shape
    bs, ts, _ = x_sample.shape
    n_heads = a_dt_bias.shape[1]
    conv0 = jnp.zeros((N_A, bp, CONV_W - 1, a_conv_w.shape[2]), F32)
    ssm0 = jnp.zeros((N_A, bp, n_heads, SSM_HEAD_DIM, D_STATE), F32)
    y_p, ssm_p, conv_p, k_p, v_p = _run_trunk(x_prompt, p_prompt, conv0, ssm0, None, None, 0, w,
                                              tm=min(512, bp * tp), tq=min(256, tp))
    y_s, ssm_s, conv_s, k_s, v_s = _run_trunk(x_sample, p_sample, state_conv, state_ssm, cache_k, cache_v,
                                              cache_k.shape[1], w, tm=min(512, bs * ts), tq=ts)
    return (y_p, y_s, ssm_p, conv_p, k_p, v_p, ssm_s, conv_s, k_s, v_s)
```

```python
import functools

import jax
import jax.numpy as jnp
from jax import lax
from jax.experimental import pallas as pl
from jax.experimental.pallas import tpu as pltpu

F32 = jnp.float32
BF16 = jnp.bfloat16

DEPTH = 4
N_A = DEPTH // 2
SSM_HEAD_DIM = 64
SSM_GROUPS = 8
D_STATE = 128
CONV_W = 4
SB_HEAD_DIM = 128
SB_BLOCK = 128
ALPHA = (2.0 * DEPTH) ** 0.25
LN_EPS = 1e-5
RMS_EPS = 1e-5

LANES = 128
SSD_CHUNK = 128
XWIN_PAD = 8
BF16_ROWS = 16
CONV_TILE = 512
ROW_GROUPS = 1
FF_CHUNK = 1024
VMEM_LIMIT = 56 * 1024 * 1024
LOG2E = 1.4426950408889634
SIGN_BIT = -2147483648
UNDERFLOW_LOG2 = 152.0


def _const_spec(shape):
    nd = len(shape)
    return pl.BlockSpec(shape, lambda *_: (0,) * nd, pipeline_mode=pl.Buffered(1))


def _dot(a, b):
    return jnp.dot(a, b, preferred_element_type=F32)


def _dot_nt(a, b):
    return lax.dot_general(a, b, (((1,), (1,)), ((), ())), preferred_element_type=F32)


def _split3(a):
    a1 = a.astype(BF16)
    r1 = a - a1.astype(F32)
    a2 = r1.astype(BF16)
    a3 = (r1 - a2.astype(F32)).astype(BF16)
    return a1, a2, a3


def _dot_f32_by01(a, m01, pieces=3):
    return sum(_dot(p, m01) for p in _split3(a)[:pieces])


def _dot_01_by_f32(m01, a):
    a1, a2, a3 = _split3(a)
    return _dot(m01, a1) + _dot(m01, a2) + _dot(m01, a3)


def _layer_norm(x, g, b):
    mu = jnp.mean(x, axis=-1, keepdims=True)
    xc = x - mu
    var = jnp.mean(xc * xc, axis=-1, keepdims=True)
    return xc * lax.rsqrt(var + LN_EPS) * g + b


def _silu(x):
    return x * jax.nn.sigmoid(x)


def _softplus(x):
    return jnp.maximum(x, 0.0) + jnp.log(1.0 + jnp.exp(-jnp.abs(x)))


def _inproj_kernel(x_ref, wz_ref, wx_ref, wd_ref, z_ref, xbc_ref, dt_ref):
    xb = x_ref[...].astype(BF16)
    z_ref[...] = _dot(xb, wz_ref[...]).astype(z_ref.dtype)
    xbc_ref[...] = _dot(xb, wx_ref[...]).astype(xbc_ref.dtype)
    dt_ref[...] = _dot(xb, wd_ref[...])


def _inproj(xf, wz, wx, wd, tm):
    n, d = xf.shape
    d_inner, conv_dim = wz.shape[1], wx.shape[1]
    return pl.pallas_call(
        _inproj_kernel,
        grid=(n // tm,),
        in_specs=[pl.BlockSpec((tm, d), lambda i: (i, 0)),
                  _const_spec(wz.shape), _const_spec(wx.shape), _const_spec(wd.shape)],
        out_specs=[pl.BlockSpec((tm, d_inner), lambda i: (i, 0)),
                   pl.BlockSpec((tm, conv_dim), lambda i: (i, 0)),
                   pl.BlockSpec((tm, LANES), lambda i: (i, 0))],
        out_shape=[jax.ShapeDtypeStruct((n, d_inner), BF16),
                   jax.ShapeDtypeStruct((n, conv_dim), BF16),
                   jax.ShapeDtypeStruct((n, LANES), F32)],
        compiler_params=pltpu.CompilerParams(dimension_semantics=("parallel",),
                                             vmem_limit_bytes=VMEM_LIMIT),
        name="inproj",
    )(xf, wz, wx, wd)


def _ssd_kernel(xbc_ref, z_ref, dtr_ref, cprev_ref, hprev_ref, cw_ref, cb_ref, dtb_ref, alog_ref,
                dskip_ref, ng_ref, expand_ref, expand_t_ref, shift_ref,
                y_ref, cnew_ref, hlast_ref,
                h_s, tail_s, xc_s, dte_s, ece_s, wee_s,
                *, chunk, t_valid, n_groups, heads_per_group, head_dim, d_state):
    L = chunk
    G, R, P, N = n_groups, heads_per_group, head_dim, d_state
    GW = R * P
    d_inner = G * GW
    c = pl.program_id(1)
    nc = pl.num_programs(1)

    TAIL = slice(XWIN_PAD - (CONV_W - 1), XWIN_PAD)

    @pl.when(c == 0)
    def _():
        h_s[...] = hprev_ref[...]
        tail_s[...] = jnp.zeros_like(tail_s)
        tail_s[TAIL, :] = cprev_ref[...]

    fix = tail_s[TAIL.start:TAIL.start + XWIN_PAD, :] * cw_ref[0:1, :]
    for kk in range(1, CONV_W - 1):
        fix = fix + tail_s[TAIL.start + kk:TAIL.start + kk + XWIN_PAD, :] * cw_ref[kk:kk + 1, :]
    conv_dim = xbc_ref.shape[1]
    for ct in range(conv_dim // CONV_TILE):
        cs = slice(ct * CONV_TILE, (ct + 1) * CONV_TILE)
        xb = xbc_ref[:, cs]
        sh = _dot(shift_ref[...], xb)
        pre = xb.astype(F32) * cw_ref[CONV_W - 1:CONV_W, cs] + cb_ref[:, cs]
        for kk in range(CONV_W - 1):
            pre = pre + sh[kk * L:(kk + 1) * L] * cw_ref[kk:kk + 1, cs]
        xc_s[0:XWIN_PAD, cs] = _silu(pre[0:XWIN_PAD] + fix[:, cs])
        xc_s[XWIN_PAD:L, cs] = _silu(pre[XWIN_PAD:L])

    dt = _softplus(dtr_ref[...] + dtb_ref[...])
    if t_valid % L:
        row = lax.broadcasted_iota(jnp.int32, (L, LANES), 0) + c * L
        dt = jnp.where(row < t_valid, dt, 0.0)
    a_neg = -jnp.exp(alog_ref[...])
    ri = lax.broadcasted_iota(jnp.int32, (L, L), 0)
    ci = lax.broadcasted_iota(jnp.int32, (L, L), 1)
    causal = ci <= ri
    tri = causal.astype(BF16)
    cum = _dot_01_by_f32(tri, dt * a_neg)
    cum_t = cum.T
    cum_last = cum[L - 1:L, :]
    per_head = (dt, jnp.exp(cum), jnp.exp(cum_last - cum) * dt)
    stacked = jnp.concatenate([p for a in per_head for p in _split3(a)[:2]], axis=0)
    for ct in range(d_inner // CONV_TILE):
        cs = slice(ct * CONV_TILE, (ct + 1) * CONV_TILE)
        ex = _dot(stacked, expand_ref[:, cs])
        for dst, j in ((dte_s, 0), (ece_s, 1), (wee_s, 2)):
            dst[:, cs] = ex[2 * j * L:(2 * j + 1) * L] + ex[(2 * j + 1) * L:(2 * j + 2) * L]
    cum_last_cols = jnp.broadcast_to(cum_t[:, L - 1:L], (LANES, N))
    dec_rows = jnp.exp(_dot_01_by_f32(expand_t_ref[...], cum_last_cols))

    lane = lax.broadcasted_iota(jnp.int32, (L, 2 * P), 1)
    lo_half = lane < P

    for g in range(G):
        gs = slice(g * GW, (g + 1) * GW)
        b_g = xc_s[:, d_inner + g * N:d_inner + (g + 1) * N].astype(BF16)
        c_g = xc_s[:, d_inner + G * N + g * N:d_inner + G * N + (g + 1) * N].astype(BF16)
        x_g = xc_s[:, gs]
        cb = jnp.where(causal, _dot_nt(c_g, b_g), 0.0)
        h_g = h_s[gs, :]
        y_off = _dot_nt(c_g, h_g.astype(BF16)) * ece_s[:, gs]
        xdt = (x_g * dte_s[:, gs]).astype(BF16)
        y_parts = []
        for pr in range(R // 2):
            ms = []
            for r in (2 * pr, 2 * pr + 1):
                hh = g * R + r
                seg = cum[:, hh:hh + 1] - cum_t[hh:hh + 1, :]
                ms.append((cb * jnp.exp(jnp.minimum(seg, 0.0))).astype(BF16))
            m2 = jnp.concatenate(ms, axis=1)
            xp = xdt[:, pr * 2 * P:(pr + 1) * 2 * P]
            zero = jnp.zeros_like(xp)
            w2 = jnp.concatenate([jnp.where(lo_half, xp, zero), jnp.where(lo_half, zero, xp)], axis=0)
            y_parts.append(_dot(m2, w2))
        y = jnp.concatenate(y_parts, axis=1) + y_off + dskip_ref[:, gs] * x_g
        zg = z_ref[:, gs].astype(F32)
        y = y * _silu(zg)
        y = y * lax.rsqrt(jnp.mean(y * y, axis=-1, keepdims=True) + RMS_EPS)
        y_ref[:, gs] = (y * ng_ref[:, gs]).astype(y_ref.dtype)
        xw_t = (x_g * wee_s[:, gs]).T.astype(BF16)
        h_s[gs, :] = h_g * dec_rows[gs, :] + _dot(xw_t, b_g)

    last_row = (t_valid - 1) % L + 1
    assert last_row % BF16_ROWS == 0
    last_rows = slice(BF16_ROWS - (CONV_W - 1), BF16_ROWS)

    @pl.when(c < nc - 1)
    def _():
        tail_s[TAIL, :] = xbc_ref[L - BF16_ROWS:L, :].astype(F32)[last_rows]

    @pl.when(c == nc - 1)
    def _():
        cnew_ref[...] = xbc_ref[last_row - BF16_ROWS:last_row, :].astype(F32)[last_rows]
        hlast_ref[...] = h_s[...]


def _ssd(xbc, z, dtr, conv_prev, h_prev, cw, cb, dtb, alog, dskip_e, ng, expand, expand_t, shift, t_valid):
    bsz, t_pad, conv_dim = xbc.shape
    d_inner = z.shape[2]
    L = SSD_CHUNK
    nc = t_pad // L
    n_heads = d_inner // SSM_HEAD_DIM
    kern = functools.partial(_ssd_kernel, chunk=L, t_valid=t_valid, n_groups=SSM_GROUPS,
                             heads_per_group=n_heads // SSM_GROUPS, head_dim=SSM_HEAD_DIM, d_state=D_STATE)
    tok = lambda w: pl.BlockSpec((None, L, w), lambda b, c: (b, c, 0))
    per_b = lambda s: pl.BlockSpec((None,) + s, lambda b, c: (b,) + (0,) * len(s))
    return pl.pallas_call(
        kern,
        grid=(bsz, nc),
        in_specs=[tok(conv_dim), tok(d_inner), tok(LANES),
                  per_b((CONV_W - 1, conv_dim)), per_b((d_inner, D_STATE)),
                  _const_spec(cw.shape), _const_spec(cb.shape), _const_spec(dtb.shape), _const_spec(alog.shape),
                  _const_spec(dskip_e.shape), _const_spec(ng.shape), _const_spec(expand.shape),
                  _const_spec(expand_t.shape), _const_spec(shift.shape)],
        out_specs=[tok(d_inner), per_b((CONV_W - 1, conv_dim)), per_b((d_inner, D_STATE))],
        out_shape=[jax.ShapeDtypeStruct((bsz, t_pad, d_inner), BF16),
                   jax.ShapeDtypeStruct((bsz, CONV_W - 1, conv_dim), F32),
                   jax.ShapeDtypeStruct((bsz, d_inner, D_STATE), F32)],
        scratch_shapes=[pltpu.VMEM((d_inner, D_STATE), F32),
                        pltpu.VMEM((2 * XWIN_PAD, conv_dim), F32),
                        pltpu.VMEM((L, conv_dim), F32),
                        pltpu.VMEM((L, d_inner), F32),
                        pltpu.VMEM((L, d_inner), F32),
                        pltpu.VMEM((L, d_inner), F32)],
        compiler_params=pltpu.CompilerParams(dimension_semantics=("parallel", "arbitrary"),
                                             vmem_limit_bytes=VMEM_LIMIT),
        name="ssd",
    )(xbc, z, dtr, conv_prev, h_prev, cw, cb, dtb, alog, dskip_e, ng, expand, expand_t, shift)


def _post_kernel(*refs, ff_chunk, with_kv):
    if with_kv:
        (x_ref, m_ref, p_ref, wo_ref, g1_ref, b1_ref, w1_ref, w2_ref, g2_ref, b2_ref, wg_ref, wp_ref,
         kg_ref, kb_ref, wkv_ref, o_ref, k_ref, v_ref, k16_ref, v16_ref) = refs
    else:
        (x_ref, m_ref, p_ref, wo_ref, g1_ref, b1_ref, w1_ref, w2_ref, g2_ref, b2_ref, wg_ref, wp_ref,
         o_ref) = refs
    tm = x_ref.shape[0]
    rows = [slice(s * (tm // ROW_GROUPS), (s + 1) * (tm // ROW_GROUPS)) for s in range(ROW_GROUPS)]
    d_ff = w1_ref.shape[1]

    def ln1(r):
        return _layer_norm(ALPHA * x_ref[r, :] + _dot(m_ref[r, :], wo_ref[...]), g1_ref[...], b1_ref[...])

    def mlp(x1):
        x1b = x1.astype(BF16)
        acc = ALPHA * x1
        for cc in range(d_ff // ff_chunk):
            cs = slice(cc * ff_chunk, (cc + 1) * ff_chunk)
            h = jnp.square(jnp.maximum(_dot(x1b, w1_ref[:, cs]), 0.0)).astype(BF16)
            acc = acc + _dot(h, w2_ref[cs, :])
        return acc

    def finish(pre2, r):
        x2 = _layer_norm(pre2, g2_ref[...], b2_ref[...])
        gate = jax.nn.sigmoid(_dot(x2.astype(BF16), wg_ref[...]))
        pe = _dot(p_ref[r, :].astype(BF16), wp_ref[...])
        x3 = x2 + gate * pe
        o_ref[r, :] = x3
        if with_kv:
            kvn = _layer_norm(x3, kg_ref[...], kb_ref[...]).astype(BF16)
            kv = _dot(kvn, wkv_ref[...])
            w = k_ref.shape[1]
            k_ref[r, :] = kv[:, :w]
            v_ref[r, :] = kv[:, w:]
            k16_ref[r, :] = kv[:, :w].astype(BF16)
            v16_ref[r, :] = kv[:, w:].astype(BF16)

    x1 = [ln1(rows[0])]
    pre2 = []
    for s in range(ROW_GROUPS):
        if s + 1 < ROW_GROUPS:
            x1.append(ln1(rows[s + 1]))
        pre2.append(mlp(x1[s]))
        if s > 0:
            finish(pre2[s - 1], rows[s - 1])
    finish(pre2[-1], rows[-1])


def _post(xf, mf, pf, wo, g1, b1, w1, w2, g2, b2, wg, wp, kv_params, tm):
    n, d = xf.shape
    km, pd = mf.shape[1], pf.shape[1]
    with_kv = kv_params is not None
    row = lambda w: pl.BlockSpec((tm, w), lambda i: (i, 0))
    ins = [xf, mf, pf, wo, g1, b1, w1, w2, g2, b2, wg, wp]
    in_specs = [row(d), row(km), row(pd)] + [_const_spec(a.shape) for a in ins[3:]]
    out_specs = [row(d)]
    out_shape = [jax.ShapeDtypeStruct((n, d), F32)]
    if with_kv:
        ins += list(kv_params)
        in_specs += [_const_spec(a.shape) for a in kv_params]
        w = kv_params[2].shape[1] // 2
        out_specs += [row(w)] * 4
        out_shape += [jax.ShapeDtypeStruct((n, w), F32)] * 2 + [jax.ShapeDtypeStruct((n, w), BF16)] * 2
    return pl.pallas_call(
        functools.partial(_post_kernel, ff_chunk=min(FF_CHUNK, w1.shape[1]), with_kv=with_kv),
        grid=(n // tm,),
        in_specs=in_specs, out_specs=out_specs, out_shape=out_shape,
        compiler_params=pltpu.CompilerParams(dimension_semantics=("parallel",),
                                             vmem_limit_bytes=VMEM_LIMIT),
        name="post_kv" if with_kv else "post",
    )(*ins)


def _attn_kernel(x_ref, wq_ref, k_ref, v_ref, ut_ref, o_ref, q_s, suf_s, acc_s,
                 *, tq, q_offset, n_heads, head_dim, scale):
    KB = SB_BLOCK
    qi = pl.program_id(1)
    q = _dot(x_ref[...].astype(BF16), wq_ref[...]) * (scale * LOG2E)
    q_s[...] = q.astype(BF16)
    suf_s[...] = jnp.zeros_like(suf_s)
    acc_s[...] = jnp.zeros_like(acc_s)
    q_min = q_offset + qi * tq
    n_full = q_min // KB
    n_diag = -(-tq // KB)
    ut = ut_ref[...]
    heads = [slice(h * head_dim, (h + 1) * head_dim) for h in range(n_heads)]

    def key_block(kstart, masked):
        if masked:
            qpos = q_min + lax.broadcasted_iota(jnp.int32, (tq, KB), 0)
            kpos = kstart + lax.broadcasted_iota(jnp.int32, (tq, KB), 1)
            mask = kpos < qpos
        z2 = [_dot_nt(q_s[:, hs], k_ref[pl.ds(kstart, KB), hs]) for hs in heads]
        sp2, lsig2 = [], []
        for z in z2:
            neg_abs = lax.bitcast_convert_type(lax.bitcast_convert_type(z, jnp.int32) | SIGN_BIT, F32)
            sp = jnp.maximum(z, 0.0) + jnp.log(1.0 + jnp.exp2(neg_abs)) * LOG2E
            lsig2.append(z - sp)
            if masked:
                sp = jnp.where(mask, sp, 0.0)
            sp2.append(sp.astype(BF16))
        it = [_dot(sp, ut) for sp in sp2]
        w = []
        for h in range(n_heads):
            wh = jnp.exp2(lsig2[h] - it[h][:, :KB] - suf_s[h])
            if masked:
                wh = jnp.where(mask, wh, 0.0)
            w.append(wh.astype(BF16))
            suf_s[h] = suf_s[h] + it[h][:, KB:]
        for h, hs in enumerate(heads):
            acc_s[:, hs] = acc_s[:, hs] + _dot(w[h], v_ref[pl.ds(kstart, KB), hs])

    def any_row_alive():
        return (jnp.min(suf_s[...]) < UNDERFLOW_LOG2).astype(jnp.int32)

    for d in reversed(range(n_diag)):
        key_block(pl.multiple_of(q_min + d * KB, KB), True)

    def cond(carry):
        i, alive = carry
        return jnp.logical_and(i < n_full, alive > 0)

    def body(carry):
        i, _ = carry
        key_block(pl.multiple_of((n_full - 1 - i) * KB, KB), False)
        return i + 1, any_row_alive()

    lax.while_loop(cond, body, (jnp.int32(0), any_row_alive()))
    o_ref[...] = acc_s[...].astype(o_ref.dtype)


def _attn(x, wq, k16, v16, ut, q_offset, tq):
    bsz, t, d = x.shape
    tk, width = k16.shape[1], k16.shape[2]
    n_heads = width // SB_HEAD_DIM
    assert q_offset % SB_BLOCK == 0 and t % tq == 0 and (tq % SB_BLOCK == 0 or t == tq)
    assert tk % SB_BLOCK == 0 and tk >= q_offset + (-(-t // SB_BLOCK)) * SB_BLOCK
    kern = functools.partial(_attn_kernel, tq=tq, q_offset=q_offset, n_heads=n_heads,
                             head_dim=SB_HEAD_DIM, scale=SB_HEAD_DIM ** -0.5)
    return pl.pallas_call(
        kern,
        grid=(bsz, t // tq),
        in_specs=[pl.BlockSpec((None, tq, d), lambda b, i: (b, i, 0)),
                  _const_spec(wq.shape),
                  pl.BlockSpec((None, tk, width), lambda b, i: (b, 0, 0)),
                  pl.BlockSpec((None, tk, width), lambda b, i: (b, 0, 0)),
                  _const_spec(ut.shape)],
        out_specs=pl.BlockSpec((None, tq, width), lambda b, i: (b, i, 0)),
        out_shape=jax.ShapeDtypeStruct((bsz, t, width), BF16),
        scratch_shapes=[pltpu.VMEM((tq, width), BF16),
                        pltpu.VMEM((n_heads, tq, SB_BLOCK), F32),
                        pltpu.VMEM((tq, width), F32)],
        compiler_params=pltpu.CompilerParams(dimension_semantics=("parallel", "arbitrary"),
                                             vmem_limit_bytes=VMEM_LIMIT),
        name="sb_attn",
    )(x, wq, k16, v16, ut)


def _row(v):
    return v.reshape(1, -1).astype(F32)


def _pad_lanes(v):
    return jnp.pad(v.reshape(1, -1).astype(F32), ((0, 0), (0, LANES - v.size)))


def _prep_weights(prm):
    d_inner = prm['a_w_out'].shape[1]
    conv_dim = prm['a_conv_w'].shape[2]
    n_heads = prm['a_dt_bias'].shape[1]
    w = dict(prm)
    w['wz'] = prm['a_w_in'][:, :, :d_inner].astype(BF16)
    w['wx'] = prm['a_w_in'][:, :, d_inner:d_inner + conv_dim].astype(BF16)
    w['wd'] = jnp.pad(prm['a_w_in'][:, :, d_inner + conv_dim:], ((0, 0), (0, 0), (0, LANES - n_heads))).astype(BF16)
    for name in ('a_w_out', 'w_kv', 'b_w_q', 'b_w_out', 'mlp_w1', 'mlp_w2', 'ple_w', 'ple_gate_w'):
        w[name] = prm[name].astype(BF16)
    head_of_channel = jnp.arange(d_inner) // SSM_HEAD_DIM
    w['expand'] = (jnp.arange(LANES)[:, None] == head_of_channel[None, :]).astype(BF16)
    tt = jnp.arange(SSD_CHUNK)
    w['shift'] = jnp.concatenate([(tt[None, :] == tt[:, None] - s).astype(BF16) for s in range(CONV_W - 1, 0, -1)], axis=0)
    idx = jnp.arange(SB_BLOCK)
    upper = (idx[:, None] > idx[None, :]).astype(BF16)
    w['ut'] = jnp.concatenate([upper, jnp.ones((SB_BLOCK, SB_BLOCK), BF16)], axis=1)
    return w


def _run_trunk(x, p, conv_prev, ssm_prev, k_past, v_past, q_offset, w, tm, tq):
    bsz, t, d = x.shape
    n = bsz * t
    d_inner = w['a_w_out'].shape[1]
    xf = x.reshape(n, d)
    conv_states, ssm_states = [], []
    k_new = v_new = k16 = v16 = None
    t_pad = -(-t // SSD_CHUNK) * SSD_CHUNK
    for i in range(DEPTH):
        if i < N_A:
            z, xbc, dtr = _inproj(xf, w['wz'][i], w['wx'][i], w['wd'][i], tm)
            z, xbc, dtr = (a.reshape(bsz, t, -1) for a in (z, xbc, dtr))
            if t_pad != t:
                z, xbc, dtr = (jnp.pad(a, ((0, 0), (0, t_pad - t), (0, 0))) for a in (z, xbc, dtr))
            y, cnew, hlast = _ssd(
                xbc, z, dtr, conv_prev[i], ssm_prev[i].reshape(bsz, d_inner, D_STATE),
                w['a_conv_w'][i], _row(w['a_conv_b'][i]), _pad_lanes(w['a_dt_bias'][i]),
                _pad_lanes(w['a_A_log'][i]), _row(jnp.repeat(w['a_d_skip'][i], SSM_HEAD_DIM)),
                _row(w['a_norm_g'][i]), w['expand'], w['expand'].T, w['shift'], t)
            conv_states.append(cnew)
            ssm_states.append(hlast.reshape(bsz, d_inner // SSM_HEAD_DIM, SSM_HEAD_DIM, D_STATE))
            mf = y[:, :t].reshape(n, d_inner)
            wo = w['a_w_out'][i]
        else:
            j = i - N_A
            o = _attn(xf.reshape(bsz, t, d), w['b_w_q'][j], k16, v16, w['ut'], q_offset, tq)
            mf = o.reshape(n, -1)
            wo = w['b_w_out'][j]
        kv_params = (_row(w['kv_norm_g']), _row(w['kv_norm_b']), w['w_kv']) if i == N_A - 1 else None
        outs = _post(xf, mf, p[i].reshape(n, -1), wo, _row(w['ln1_g'][i]), _row(w['ln1_b'][i]),
                     w['mlp_w1'][i], w['mlp_w2'][i], _row(w['ln2_g'][i]), _row(w['ln2_b'][i]),
                     w['ple_gate_w'][i], w['ple_w'][i], kv_params, tm)
        xf = outs[0]
        if kv_params is not None:
            k_new, v_new, kb, vb = outs[1:]
            width = kb.shape[1]
            k16, v16 = kb.reshape(bsz, t, width), vb.reshape(bsz, t, width)
            if k_past is not None:
                k16 = jnp.concatenate([k_past.reshape(bsz, -1, width).astype(BF16), k16], axis=1)
                v16 = jnp.concatenate([v_past.reshape(bsz, -1, width).astype(BF16), v16], axis=1)
            tk = k16.shape[1]
            tk_pad = -(-tk // SB_BLOCK) * SB_BLOCK
            if tk_pad != tk:
                k16 = jnp.pad(k16, ((0, 0), (0, tk_pad - tk), (0, 0)))
                v16 = jnp.pad(v16, ((0, 0), (0, tk_pad - tk), (0, 0)))
    n_sb = k_new.shape[1] // SB_HEAD_DIM
    return (xf.reshape(bsz, t, d), jnp.stack(ssm_states), jnp.stack(conv_states),
            k_new.reshape(bsz, t, n_sb, SB_HEAD_DIM), v_new.reshape(bsz, t, n_sb, SB_HEAD_DIM))


def kernel(x_prompt, x_sample, p_prompt, p_sample, state_ssm, state_conv, cache_k, cache_v, a_w_in, a_conv_w, a_conv_b, a_dt_bias, a_A_log, a_d_skip, a_norm_g, a_w_out, kv_norm_g, kv_norm_b, w_kv, b_w_q, b_w_out, ln1_g, ln1_b, ln2_g, ln2_b, mlp_w1, mlp_w2, ple_w, ple_gate_w):
    prm = {
        'a_w_in': a_w_in, 'a_conv_w': a_conv_w, 'a_conv_b': a_conv_b, 'a_dt_bias': a_dt_bias,
        'a_A_log': a_A_log, 'a_d_skip': a_d_skip, 'a_norm_g': a_norm_g, 'a_w_out': a_w_out,
        'kv_norm_g': kv_norm_g, 'kv_norm_b': kv_norm_b, 'w_kv': w_kv,
        'b_w_q': b_w_q, 'b_w_out': b_w_out,
        'ln1_g': ln1_g, 'ln1_b': ln1_b, 'ln2_g': ln2_g, 'ln2_b': ln2_b,
        'mlp_w1': mlp_w1, 'mlp_w2': mlp_w2, 'ple_w': ple_w, 'ple_gate_w': ple_gate_w,
    }
    w = _prep_weights(prm)
    bp, tp, _ = x_prompt.shape
    bs, ts, _ = x_sample.shape
    n_heads = a_dt_bias.shape[1]
    conv0 = jnp.zeros((N_A, bp, CONV_W - 1, a_conv_w.shape[2]), F32)
    ssm0 = jnp.zeros((N_A, bp, n_heads, SSM_HEAD_DIM, D_STATE), F32)
    y_p, ssm_p, conv_p, k_p, v_p = _run_trunk(x_prompt, p_prompt, conv0, ssm0, None, None, 0, w,
                                              tm=min(512, bp * tp), tq=min(256, tp))
    y_s, ssm_s, conv_s, k_s, v_s = _run_trunk(x_sample, p_sample, state_conv, state_ssm, cache_k, cache_v,
                                              cache_k.shape[1], w, tm=min(512, bs * ts), tq=ts)
    return (y_p, y_s, ssm_p, conv_p, k_p, v_p, ssm_s, conv_s, k_s, v_s)
```

```python
import functools

import jax
import jax.numpy as jnp
from jax import lax
from jax.experimental import pallas as pl
from jax.experimental.pallas import tpu as pltpu

F32 = jnp.float32
BF16 = jnp.bfloat16

DEPTH = 4
N_A = DEPTH // 2
SSM_HEAD_DIM = 64
SSM_GROUPS = 8
D_STATE = 128
CONV_W = 4
SB_HEAD_DIM = 128
SB_BLOCK = 128
ALPHA = (2.0 * DEPTH) ** 0.25
LN_EPS = 1e-5
RMS_EPS = 1e-5

LANES = 128
SSD_CHUNK = 128
XWIN_PAD = 8
BF16_ROWS = 16
CONV_TILE = 512
ROW_GROUPS = 2
FF_CHUNK = 1024
VMEM_LIMIT = 56 * 1024 * 1024
LOG2E = 1.4426950408889634
SIGN_BIT = -2147483648
UNDERFLOW_LOG2 = 136.0


def _const_spec(shape):
    nd = len(shape)
    return pl.BlockSpec(shape, lambda *_: (0,) * nd, pipeline_mode=pl.Buffered(1))


def _dot(a, b):
    return jnp.dot(a, b, preferred_element_type=F32)


def _dot_nt(a, b):
    return lax.dot_general(a, b, (((1,), (1,)), ((), ())), preferred_element_type=F32)


def _split3(a):
    a1 = a.astype(BF16)
    r1 = a - a1.astype(F32)
    a2 = r1.astype(BF16)
    a3 = (r1 - a2.astype(F32)).astype(BF16)
    return a1, a2, a3


def _dot_f32_by01(a, m01, pieces=3):
    return sum(_dot(p, m01) for p in _split3(a)[:pieces])


def _dot_01_by_f32(m01, a):
    a1, a2, a3 = _split3(a)
    return _dot(m01, a1) + _dot(m01, a2) + _dot(m01, a3)


def _layer_norm(x, g, b):
    mu = jnp.mean(x, axis=-1, keepdims=True)
    xc = x - mu
    var = jnp.mean(xc * xc, axis=-1, keepdims=True)
    return xc * lax.rsqrt(var + LN_EPS) * g + b


def _silu(x):
    return x * jax.nn.sigmoid(x)


def _softplus(x):
    return jnp.maximum(x, 0.0) + jnp.log(1.0 + jnp.exp(-jnp.abs(x)))


def _inproj_kernel(x_ref, wz_ref, wx_ref, wd_ref, z_ref, xbc_ref, dt_ref):
    xb = x_ref[...].astype(BF16)
    z_ref[...] = _dot(xb, wz_ref[...]).astype(z_ref.dtype)
    xbc_ref[...] = _dot(xb, wx_ref[...]).astype(xbc_ref.dtype)
    dt_ref[...] = _dot(xb, wd_ref[...])


def _inproj(xf, wz, wx, wd, tm):
    n, d = xf.shape
    d_inner, conv_dim = wz.shape[1], wx.shape[1]
    return pl.pallas_call(
        _inproj_kernel,
        grid=(n // tm,),
        in_specs=[pl.BlockSpec((tm, d), lambda i: (i, 0)),
                  _const_spec(wz.shape), _const_spec(wx.shape), _const_spec(wd.shape)],
        out_specs=[pl.BlockSpec((tm, d_inner), lambda i: (i, 0)),
                   pl.BlockSpec((tm, conv_dim), lambda i: (i, 0)),
                   pl.BlockSpec((tm, LANES), lambda i: (i, 0))],
        out_shape=[jax.ShapeDtypeStruct((n, d_inner), BF16),
                   jax.ShapeDtypeStruct((n, conv_dim), BF16),
                   jax.ShapeDtypeStruct((n, LANES), F32)],
        compiler_params=pltpu.CompilerParams(dimension_semantics=("parallel",),
                                             vmem_limit_bytes=VMEM_LIMIT),
        name="inproj",
    )(xf, wz, wx, wd)


def _ssd_kernel(xbc_ref, z_ref, dtr_ref, cprev_ref, hprev_ref, cw_ref, cb_ref, dtb_ref, alog_ref,
                dskip_ref, ng_ref, expand_ref, expand_t_ref, shift_ref,
                y_ref, cnew_ref, hlast_ref,
                h_s, tail_s, xc_s, dte_s, ece_s, wee_s,
                *, chunk, t_valid, n_groups, heads_per_group, head_dim, d_state):
    L = chunk
    G, R, P, N = n_groups, heads_per_group, head_dim, d_state
    GW = R * P
    d_inner = G * GW
    c = pl.program_id(1)
    nc = pl.num_programs(1)

    TAIL = slice(XWIN_PAD - (CONV_W - 1), XWIN_PAD)

    @pl.when(c == 0)
    def _():
        h_s[...] = hprev_ref[...]
        tail_s[...] = jnp.zeros_like(tail_s)
        tail_s[TAIL, :] = cprev_ref[...]

    fix = tail_s[TAIL.start:TAIL.start + XWIN_PAD, :] * cw_ref[0:1, :]
    for kk in range(1, CONV_W - 1):
        fix = fix + tail_s[TAIL.start + kk:TAIL.start + kk + XWIN_PAD, :] * cw_ref[kk:kk + 1, :]
    conv_dim = xbc_ref.shape[1]
    for ct in range(conv_dim // CONV_TILE):
        cs = slice(ct * CONV_TILE, (ct + 1) * CONV_TILE)
        xb = xbc_ref[:, cs]
        sh = _dot(shift_ref[...], xb)
        pre = xb.astype(F32) * cw_ref[CONV_W - 1:CONV_W, cs] + cb_ref[:, cs]
        for kk in range(CONV_W - 1):
            pre = pre + sh[kk * L:(kk + 1) * L] * cw_ref[kk:kk + 1, cs]
        xc_s[0:XWIN_PAD, cs] = _silu(pre[0:XWIN_PAD] + fix[:, cs])
        xc_s[XWIN_PAD:L, cs] = _silu(pre[XWIN_PAD:L])

    dt = _softplus(dtr_ref[...] + dtb_ref[...])
    if t_valid % L:
        row = lax.broadcasted_iota(jnp.int32, (L, LANES), 0) + c * L
        dt = jnp.where(row < t_valid, dt, 0.0)
    a_neg = -jnp.exp(alog_ref[...])
    ri = lax.broadcasted_iota(jnp.int32, (L, L), 0)
    ci = lax.broadcasted_iota(jnp.int32, (L, L), 1)
    causal = ci <= ri
    tri = causal.astype(BF16)
    cum = _dot_01_by_f32(tri, dt * a_neg)
    cum_t = cum.T
    cum_last = cum[L - 1:L, :]
    per_head = (dt, jnp.exp(cum), jnp.exp(cum_last - cum) * dt)
    stacked = jnp.concatenate([p for a in per_head for p in _split3(a)[:2]], axis=0)
    for ct in range(d_inner // CONV_TILE):
        cs = slice(ct * CONV_TILE, (ct + 1) * CONV_TILE)
        ex = _dot(stacked, expand_ref[:, cs])
        for dst, j in ((dte_s, 0), (ece_s, 1), (wee_s, 2)):
            dst[:, cs] = ex[2 * j * L:(2 * j + 1) * L] + ex[(2 * j + 1) * L:(2 * j + 2) * L]
    cum_last_cols = jnp.broadcast_to(cum_t[:, L - 1:L], (LANES, N))
    dec_rows = jnp.exp(_dot_01_by_f32(expand_t_ref[...], cum_last_cols))

    lane = lax.broadcasted_iota(jnp.int32, (L, 2 * P), 1)
    lo_half = lane < P

    for g in range(G):
        gs = slice(g * GW, (g + 1) * GW)
        b_g = xc_s[:, d_inner + g * N:d_inner + (g + 1) * N].astype(BF16)
        c_g = xc_s[:, d_inner + G * N + g * N:d_inner + G * N + (g + 1) * N].astype(BF16)
        x_g = xc_s[:, gs]
        cb = jnp.where(causal, _dot_nt(c_g, b_g), 0.0)
        h_g = h_s[gs, :]
        y_off = _dot_nt(c_g, h_g.astype(BF16)) * ece_s[:, gs]
        xdt = (x_g * dte_s[:, gs]).astype(BF16)
        y_parts = []
        for pr in range(R // 2):
            ms = []
            for r in (2 * pr, 2 * pr + 1):
                hh = g * R + r
                seg = cum[:, hh:hh + 1] - cum_t[hh:hh + 1, :]
                ms.append((cb * jnp.exp(jnp.minimum(seg, 0.0))).astype(BF16))
            m2 = jnp.concatenate(ms, axis=1)
            xp = xdt[:, pr * 2 * P:(pr + 1) * 2 * P]
            zero = jnp.zeros_like(xp)
            w2 = jnp.concatenate([jnp.where(lo_half, xp, zero), jnp.where(lo_half, zero, xp)], axis=0)
            y_parts.append(_dot(m2, w2))
        y = jnp.concatenate(y_parts, axis=1) + y_off + dskip_ref[:, gs] * x_g
        zg = z_ref[:, gs].astype(F32)
        y = y * _silu(zg)
        y = y * lax.rsqrt(jnp.mean(y * y, axis=-1, keepdims=True) + RMS_EPS)
        y_ref[:, gs] = (y * ng_ref[:, gs]).astype(y_ref.dtype)
        xw_t = (x_g * wee_s[:, gs]).T.astype(BF16)
        h_s[gs, :] = h_g * dec_rows[gs, :] + _dot(xw_t, b_g)

    last_row = (t_valid - 1) % L + 1
    assert last_row % BF16_ROWS == 0
    last_rows = slice(BF16_ROWS - (CONV_W - 1), BF16_ROWS)

    @pl.when(c < nc - 1)
    def _():
        tail_s[TAIL, :] = xbc_ref[L - BF16_ROWS:L, :].astype(F32)[last_rows]

    @pl.when(c == nc - 1)
    def _():
        cnew_ref[...] = xbc_ref[last_row - BF16_ROWS:last_row, :].astype(F32)[last_rows]
        hlast_ref[...] = h_s[...]


def _ssd(xbc, z, dtr, conv_prev, h_prev, cw, cb, dtb, alog, dskip_e, ng, expand, expand_t, shift, t_valid):
    bsz, t_pad, conv_dim = xbc.shape
    d_inner = z.shape[2]
    L = SSD_CHUNK
    nc = t_pad // L
    n_heads = d_inner // SSM_HEAD_DIM
    kern = functools.partial(_ssd_kernel, chunk=L, t_valid=t_valid, n_groups=SSM_GROUPS,
                             heads_per_group=n_heads // SSM_GROUPS, head_dim=SSM_HEAD_DIM, d_state=D_STATE)
    tok = lambda w: pl.BlockSpec((None, L, w), lambda b, c: (b, c, 0))
    per_b = lambda s: pl.BlockSpec((None,) + s, lambda b, c: (b,) + (0,) * len(s))
    return pl.pallas_call(
        kern,
        grid=(bsz, nc),
        in_specs=[tok(conv_dim), tok(d_inner), tok(LANES),
                  per_b((CONV_W - 1, conv_dim)), per_b((d_inner, D_STATE)),
                  _const_spec(cw.shape), _const_spec(cb.shape), _const_spec(dtb.shape), _const_spec(alog.shape),
                  _const_spec(dskip_e.shape), _const_spec(ng.shape), _const_spec(expand.shape),
                  _const_spec(expand_t.shape), _const_spec(shift.shape)],
        out_specs=[tok(d_inner), per_b((CONV_W - 1, conv_dim)), per_b((d_inner, D_STATE))],
        out_shape=[jax.ShapeDtypeStruct((bsz, t_pad, d_inner), BF16),
                   jax.ShapeDtypeStruct((bsz, CONV_W - 1, conv_dim), F32),
                   jax.ShapeDtypeStruct((bsz, d_inner, D_STATE), F32)],
        scratch_shapes=[pltpu.VMEM((d_inner, D_STATE), F32),
                        pltpu.VMEM((2 * XWIN_PAD, conv_dim), F32),
                        pltpu.VMEM((L, conv_dim), F32),
                        pltpu.VMEM((L, d_inner), F32),
                        pltpu.VMEM((L, d_inner), F32),
                        pltpu.VMEM((L, d_inner), F32)],
        compiler_params=pltpu.CompilerParams(dimension_semantics=("parallel", "arbitrary"),
                                             vmem_limit_bytes=VMEM_LIMIT),
        name="ssd",
    )(xbc, z, dtr, conv_prev, h_prev, cw, cb, dtb, alog, dskip_e, ng, expand, expand_t, shift)


def _post_kernel(*refs, ff_chunk, with_kv, row_groups):
    if with_kv:
        (x_ref, m_ref, p_ref, wo_ref, g1_ref, b1_ref, w1_ref, w2_ref, g2_ref, b2_ref, wg_ref, wp_ref,
         kg_ref, kb_ref, wkv_ref, o_ref, k_ref, v_ref, k16_ref, v16_ref) = refs
    else:
        (x_ref, m_ref, p_ref, wo_ref, g1_ref, b1_ref, w1_ref, w2_ref, g2_ref, b2_ref, wg_ref, wp_ref,
         o_ref) = refs
    tm = x_ref.shape[0]
    rows = [slice(s * (tm // row_groups), (s + 1) * (tm // row_groups)) for s in range(row_groups)]
    d_ff = w1_ref.shape[1]

    def ln1(r):
        return _layer_norm(ALPHA * x_ref[r, :] + _dot(m_ref[r, :], wo_ref[...]), g1_ref[...], b1_ref[...])

    def mlp(x1):
        x1b = x1.astype(BF16)
        acc = ALPHA * x1
        for cc in range(d_ff // ff_chunk):
            cs = slice(cc * ff_chunk, (cc + 1) * ff_chunk)
            h = jnp.square(jnp.maximum(_dot(x1b, w1_ref[:, cs]), 0.0)).astype(BF16)
            acc = acc + _dot(h, w2_ref[cs, :])
        return acc

    def finish(pre2, r):
        x2 = _layer_norm(pre2, g2_ref[...], b2_ref[...])
        gate = jax.nn.sigmoid(_dot(x2.astype(BF16), wg_ref[...]))
        pe = _dot(p_ref[r, :].astype(BF16), wp_ref[...])
        x3 = x2 + gate * pe
        o_ref[r, :] = x3
        if with_kv:
            kvn = _layer_norm(x3, kg_ref[...], kb_ref[...]).astype(BF16)
            kv = _dot(kvn, wkv_ref[...])
            w = k_ref.shape[1]
            k_ref[r, :] = kv[:, :w]
            v_ref[r, :] = kv[:, w:]
            k16_ref[r, :] = kv[:, :w].astype(BF16)
            v16_ref[r, :] = kv[:, w:].astype(BF16)

    x1 = [ln1(rows[0])]
    pre2 = []
    for s in range(row_groups):
        if s + 1 < row_groups:
            x1.append(ln1(rows[s + 1]))
        pre2.append(mlp(x1[s]))
        if s > 0:
            finish(pre2[s - 1], rows[s - 1])
    finish(pre2[-1], rows[-1])


def _post(xf, mf, pf, wo, g1, b1, w1, w2, g2, b2, wg, wp, kv_params, tm):
    n, d = xf.shape
    km, pd = mf.shape[1], pf.shape[1]
    with_kv = kv_params is not None
    row = lambda w: pl.BlockSpec((tm, w), lambda i: (i, 0))
    ins = [xf, mf, pf, wo, g1, b1, w1, w2, g2, b2, wg, wp]
    in_specs = [row(d), row(km), row(pd)] + [_const_spec(a.shape) for a in ins[3:]]
    out_specs = [row(d)]
    out_shape = [jax.ShapeDtypeStruct((n, d), F32)]
    if with_kv:
        ins += list(kv_params)
        in_specs += [_const_spec(a.shape) for a in kv_params]
        w = kv_params[2].shape[1] // 2
        out_specs += [row(w)] * 4
        out_shape += [jax.ShapeDtypeStruct((n, w), F32)] * 2 + [jax.ShapeDtypeStruct((n, w), BF16)] * 2
    return pl.pallas_call(
        functools.partial(_post_kernel, ff_chunk=min(FF_CHUNK, w1.shape[1]), with_kv=with_kv,
                          row_groups=1 if with_kv else ROW_GROUPS),
        grid=(n // tm,),
        in_specs=in_specs, out_specs=out_specs, out_shape=out_shape,
        compiler_params=pltpu.CompilerParams(dimension_semantics=("parallel",),
                                             vmem_limit_bytes=VMEM_LIMIT),
        name="post_kv" if with_kv else "post",
    )(*ins)


def _attn_kernel(x_ref, wq_ref, k_ref, v_ref, ut_ref, o_ref, q_s, suf_s, acc_s,
                 *, tq, q_offset, n_heads, head_dim, scale):
    KB = SB_BLOCK
    qi = pl.program_id(1)
    q = _dot(x_ref[...].astype(BF16), wq_ref[...]) * (scale * LOG2E)
    q_s[...] = q.astype(BF16)
    suf_s[...] = jnp.zeros_like(suf_s)
    acc_s[...] = jnp.zeros_like(acc_s)
    q_min = q_offset + qi * tq
    n_full = q_min // KB
    n_diag = -(-tq // KB)
    ut = ut_ref[...]
    heads = [slice(h * head_dim, (h + 1) * head_dim) for h in range(n_heads)]

    def key_block(kstart, masked, row_lo=0):
        rows = slice(row_lo, tq)
        if masked:
            qpos = q_min + row_lo + lax.broadcasted_iota(jnp.int32, (tq - row_lo, KB), 0)
            kpos = kstart + lax.broadcasted_iota(jnp.int32, (tq - row_lo, KB), 1)
            mask = kpos < qpos
        z2 = [_dot_nt(q_s[rows, hs], k_ref[pl.ds(kstart, KB), hs]) for hs in heads]
        sp2, lsig2 = [], []
        for z in z2:
            neg_abs = lax.bitcast_convert_type(lax.bitcast_convert_type(z, jnp.int32) | SIGN_BIT, F32)
            sp = jnp.maximum(z, 0.0) + jnp.log(1.0 + jnp.exp2(neg_abs)) * LOG2E
            lsig2.append(z - sp)
            if masked:
                sp = jnp.where(mask, sp, 0.0)
            sp2.append(sp.astype(BF16))
        it = [_dot(sp, ut) for sp in sp2]
        w = []
        for h in range(n_heads):
            wh = jnp.exp2(lsig2[h] - it[h][:, :KB] - suf_s[h, rows, :])
            if masked:
                wh = jnp.where(mask, wh, 0.0)
            w.append(wh.astype(BF16))
            suf_s[h, rows, :] = suf_s[h, rows, :] + it[h][:, KB:]
        for h, hs in enumerate(heads):
            acc_s[rows, hs] = acc_s[rows, hs] + _dot(w[h], v_ref[pl.ds(kstart, KB), hs])

    def any_row_alive():
        return (jnp.min(suf_s[...]) < UNDERFLOW_LOG2).astype(jnp.int32)

    for d in reversed(range(n_diag)):
        key_block(pl.multiple_of(q_min + d * KB, KB), True, row_lo=d * KB)

    def cond(carry):
        i, alive = carry
        return jnp.logical_and(i < n_full, alive > 0)

    def body(carry):
        i, _ = carry
        key_block(pl.multiple_of((n_full - 1 - i) * KB, KB), False)
        return i + 1, any_row_alive()

    lax.while_loop(cond, body, (jnp.int32(0), any_row_alive()))
    o_ref[...] = acc_s[...].astype(o_ref.dtype)


def _attn(x, wq, k16, v16, ut, q_offset, tq):
    bsz, t, d = x.shape
    tk, width = k16.shape[1], k16.shape[2]
    n_heads = width // SB_HEAD_DIM
    assert q_offset % SB_BLOCK == 0 and t % tq == 0 and (tq % SB_BLOCK == 0 or t == tq)
    assert tk % SB_BLOCK == 0 and tk >= q_offset + (-(-t // SB_BLOCK)) * SB_BLOCK
    kern = functools.partial(_attn_kernel, tq=tq, q_offset=q_offset, n_heads=n_heads,
                             head_dim=SB_HEAD_DIM, scale=SB_HEAD_DIM ** -0.5)
    return pl.pallas_call(
        kern,
        grid=(bsz, t // tq),
        in_specs=[pl.BlockSpec((None, tq, d), lambda b, i: (b, i, 0)),
                  _const_spec(wq.shape),
                  pl.BlockSpec((None, tk, width), lambda b, i: (b, 0, 0)),
                  pl.BlockSpec((None, tk, width), lambda b, i: (b, 0, 0)),
                  _const_spec(ut.shape)],
        out_specs=pl.BlockSpec((None, tq, width), lambda b, i: (b, i, 0)),
        out_shape=jax.ShapeDtypeStruct((bsz, t, width), BF16),
        scratch_shapes=[pltpu.VMEM((tq, width), BF16),
                        pltpu.VMEM((n_heads, tq, SB_BLOCK), F32),
                        pltpu.VMEM((tq, width), F32)],
        compiler_params=pltpu.CompilerParams(dimension_semantics=("parallel", "arbitrary"),
                                             vmem_limit_bytes=VMEM_LIMIT),
        name="sb_attn",
    )(x, wq, k16, v16, ut)


def _row(v):
    return v.reshape(1, -1).astype(F32)


def _pad_lanes(v):
    return jnp.pad(v.reshape(1, -1).astype(F32), ((0, 0), (0, LANES - v.size)))


def _prep_weights(prm):
    d_inner = prm['a_w_out'].shape[1]
    conv_dim = prm['a_conv_w'].shape[2]
    n_heads = prm['a_dt_bias'].shape[1]
    w = dict(prm)
    w['wz'] = prm['a_w_in'][:, :, :d_inner].astype(BF16)
    w['wx'] = prm['a_w_in'][:, :, d_inner:d_inner + conv_dim].astype(BF16)
    w['wd'] = jnp.pad(prm['a_w_in'][:, :, d_inner + conv_dim:], ((0, 0), (0, 0), (0, LANES - n_heads))).astype(BF16)
    for name in ('a_w_out', 'w_kv', 'b_w_q', 'b_w_out', 'mlp_w1', 'mlp_w2', 'ple_w', 'ple_gate_w'):
        w[name] = prm[name].astype(BF16)
    head_of_channel = jnp.arange(d_inner) // SSM_HEAD_DIM
    w['expand'] = (jnp.arange(LANES)[:, None] == head_of_channel[None, :]).astype(BF16)
    tt = jnp.arange(SSD_CHUNK)
    w['shift'] = jnp.concatenate([(tt[None, :] == tt[:, None] - s).astype(BF16) for s in range(CONV_W - 1, 0, -1)], axis=0)
    idx = jnp.arange(SB_BLOCK)
    upper = (idx[:, None] > idx[None, :]).astype(BF16)
    w['ut'] = jnp.concatenate([upper, jnp.ones((SB_BLOCK, SB_BLOCK), BF16)], axis=1)
    return w


def _run_trunk(x, p, conv_prev, ssm_prev, k_past, v_past, q_offset, w, tm, tq):
    bsz, t, d = x.shape
    n = bsz * t
    d_inner = w['a_w_out'].shape[1]
    xf = x.reshape(n, d)
    conv_states, ssm_states = [], []
    k_new = v_new = k16 = v16 = None
    t_pad = -(-t // SSD_CHUNK) * SSD_CHUNK
    for i in range(DEPTH):
        if i < N_A:
            z, xbc, dtr = _inproj(xf, w['wz'][i], w['wx'][i], w['wd'][i], tm)
            z, xbc, dtr = (a.reshape(bsz, t, -1) for a in (z, xbc, dtr))
            if t_pad != t:
                z, xbc, dtr = (jnp.pad(a, ((0, 0), (0, t_pad - t), (0, 0))) for a in (z, xbc, dtr))
            y, cnew, hlast = _ssd(
                xbc, z, dtr, conv_prev[i], ssm_prev[i].reshape(bsz, d_inner, D_STATE),
                w['a_conv_w'][i], _row(w['a_conv_b'][i]), _pad_lanes(w['a_dt_bias'][i]),
                _pad_lanes(w['a_A_log'][i]), _row(jnp.repeat(w['a_d_skip'][i], SSM_HEAD_DIM)),
                _row(w['a_norm_g'][i]), w['expand'], w['expand'].T, w['shift'], t)
            conv_states.append(cnew)
            ssm_states.append(hlast.reshape(bsz, d_inner // SSM_HEAD_DIM, SSM_HEAD_DIM, D_STATE))
            mf = y[:, :t].reshape(n, d_inner)
            wo = w['a_w_out'][i]
        else:
            j = i - N_A
            o = _attn(xf.reshape(bsz, t, d), w['b_w_q'][j], k16, v16, w['ut'], q_offset, tq)
            mf = o.reshape(n, -1)
            wo = w['b_w_out'][j]
        kv_params = (_row(w['kv_norm_g']), _row(w['kv_norm_b']), w['w_kv']) if i == N_A - 1 else None
        outs = _post(xf, mf, p[i].reshape(n, -1), wo, _row(w['ln1_g'][i]), _row(w['ln1_b'][i]),
                     w['mlp_w1'][i], w['mlp_w2'][i], _row(w['ln2_g'][i]), _row(w['ln2_b'][i]),
                     w['ple_gate_w'][i], w['ple_w'][i], kv_params, tm)
        xf = outs[0]
        if kv_params is not None:
            k_new, v_new, kb, vb = outs[1:]
            width = kb.shape[1]
            k16, v16 = kb.reshape(bsz, t, width), vb.reshape(bsz, t, width)
            if k_past is not None:
                k16 = jnp.concatenate([k_past.reshape(bsz, -1, width).astype(BF16), k16], axis=1)
                v16 = jnp.concatenate([v_past.reshape(bsz, -1, width).astype(BF16), v16], axis=1)
            tk = k16.shape[1]
            tk_pad = -(-tk // SB_BLOCK) * SB_BLOCK
            if tk_pad != tk:
                k16 = jnp.pad(k16, ((0, 0), (0, tk_pad - tk), (0, 0)))
                v16 = jnp.pad(v16, ((0, 0), (0, tk_pad - tk), (0, 0)))
    n_sb = k_new.shape[1] // SB_HEAD_DIM
    return (xf.reshape(bsz, t, d), jnp.stack(ssm_states), jnp.stack(conv_states),
            k_new.reshape(bsz, t, n_sb, SB_HEAD_DIM), v_new.reshape(bsz, t, n_sb, SB_HEAD_DIM))


def kernel(x_prompt, x_sample, p_prompt, p_sample, state_ssm, state_conv, cache_k, cache_v, a_w_in, a_conv_w, a_conv_b, a_dt_bias, a_A_log, a_d_skip, a_norm_g, a_w_out, kv_norm_g, kv_norm_b, w_kv, b_w_q, b_w_out, ln1_g, ln1_b, ln2_g, ln2_b, mlp_w1, mlp_w2, ple_w, ple_gate_w):
    prm = {
        'a_w_in': a_w_in, 'a_conv_w': a_conv_w, 'a_conv_b': a_conv_b, 'a_dt_bias': a_dt_bias,
        'a_A_log': a_A_log, 'a_d_skip': a_d_skip, 'a_norm_g': a_norm_g, 'a_w_out': a_w_out,
        'kv_norm_g': kv_norm_g, 'kv_norm_b': kv_norm_b, 'w_kv': w_kv,
        'b_w_q': b_w_q, 'b_w_out': b_w_out,
        'ln1_g': ln1_g, 'ln1_b': ln1_b, 'ln2_g': ln2_g, 'ln2_b': ln2_b,
        'mlp_w1': mlp_w1, 'mlp_w2': mlp_w2, 'ple_w': ple_w, 'ple_gate_w': ple_gate_w,
    }
    w = _prep_weights(prm)
    bp, tp, _ = x_prompt.shape
    bs, ts, _ = x_sample.shape
    n_heads = a_dt_bias.shape[1]
    conv0 = jnp.zeros((N_A, bp, CONV_W - 1, a_conv_w.shape[2]), F32)
    ssm0 = jnp.zeros((N_A, bp, n_heads, SSM_HEAD_DIM, D_STATE), F32)
    y_p, ssm_p, conv_p, k_p, v_p = _run_trunk(x_prompt, p_prompt, conv0, ssm0, None, None, 0, w,
                                              tm=min(512, bp * tp), tq=min(256, tp))
    y_s, ssm_s, conv_s, k_s, v_s = _run_trunk(x_sample, p_sample, state_conv, state_ssm, cache_k, cache_v,
                                              cache_k.shape[1], w, tm=min(512, bs * ts), tq=ts)
    return (y_p, y_s, ssm_p, conv_p, k_p, v_p, ssm_s, conv_s, k_s, v_s)
```

```python
import functools

import jax
import jax.numpy as jnp
from jax import lax
from jax.experimental import pallas as pl
from jax.experimental.pallas import tpu as pltpu

F32 = jnp.float32
BF16 = jnp.bfloat16

DEPTH = 4
N_A = DEPTH // 2
SSM_HEAD_DIM = 64
SSM_GROUPS = 8
D_STATE = 128
CONV_W = 4
SB_HEAD_DIM = 128
SB_BLOCK = 128
ALPHA = (2.0 * DEPTH) ** 0.25
LN_EPS = 1e-5
RMS_EPS = 1e-5

LANES = 128
SSD_CHUNK = 128
XWIN_PAD = 8
BF16_ROWS = 16
CONV_TILE = 256
GROUP_BATCH = 2
INPROJ_ROWS = 1024
ROW_GROUPS = 2
FF_CHUNK = 1024
VMEM_LIMIT = 56 * 1024 * 1024
LOG2E = 1.4426950408889634
SIGN_BIT = -2147483648
UNDERFLOW_LOG2 = 136.0


def _const_spec(shape):
    nd = len(shape)
    return pl.BlockSpec(shape, lambda *_: (0,) * nd, pipeline_mode=pl.Buffered(1))


def _dot(a, b):
    return jnp.dot(a, b, preferred_element_type=F32)


def _dot_nt(a, b):
    return lax.dot_general(a, b, (((1,), (1,)), ((), ())), preferred_element_type=F32)


def _split3(a):
    a1 = a.astype(BF16)
    r1 = a - a1.astype(F32)
    a2 = r1.astype(BF16)
    a3 = (r1 - a2.astype(F32)).astype(BF16)
    return a1, a2, a3


def _dot_f32_by01(a, m01, pieces=3):
    return sum(_dot(p, m01) for p in _split3(a)[:pieces])


def _dot_01_by_f32(m01, a):
    a1, a2, a3 = _split3(a)
    return _dot(m01, a1) + _dot(m01, a2) + _dot(m01, a3)


def _layer_norm(x, g, b):
    mu = jnp.mean(x, axis=-1, keepdims=True)
    xc = x - mu
    var = jnp.mean(xc * xc, axis=-1, keepdims=True)
    return xc * lax.rsqrt(var + LN_EPS) * g + b


def _silu(x):
    return x * jax.nn.sigmoid(x)


def _softplus(x):
    return jnp.maximum(x, 0.0) + jnp.log(1.0 + jnp.exp(-jnp.abs(x)))


def _inproj_kernel(x_ref, wz_ref, wx_ref, wd_ref, z_ref, xbc_ref, dt_ref):
    xb = x_ref[...].astype(BF16)
    z_ref[...] = _dot(xb, wz_ref[...]).astype(z_ref.dtype)
    xbc_ref[...] = _dot(xb, wx_ref[...]).astype(xbc_ref.dtype)
    dt_ref[...] = _dot(xb, wd_ref[...])


def _inproj(xf, wz, wx, wd, tm):
    n, d = xf.shape
    d_inner, conv_dim = wz.shape[1], wx.shape[1]
    return pl.pallas_call(
        _inproj_kernel,
        grid=(n // tm,),
        in_specs=[pl.BlockSpec((tm, d), lambda i: (i, 0)),
                  _const_spec(wz.shape), _const_spec(wx.shape), _const_spec(wd.shape)],
        out_specs=[pl.BlockSpec((tm, d_inner), lambda i: (i, 0)),
                   pl.BlockSpec((tm, conv_dim), lambda i: (i, 0)),
                   pl.BlockSpec((tm, LANES), lambda i: (i, 0))],
        out_shape=[jax.ShapeDtypeStruct((n, d_inner), BF16),
                   jax.ShapeDtypeStruct((n, conv_dim), BF16),
                   jax.ShapeDtypeStruct((n, LANES), F32)],
        compiler_params=pltpu.CompilerParams(dimension_semantics=("parallel",),
                                             vmem_limit_bytes=VMEM_LIMIT),
        name="inproj",
    )(xf, wz, wx, wd)


def _ssd_kernel(xbc_ref, z_ref, dtr_ref, cprev_ref, hprev_ref, cw_ref, cb_ref, dtb_ref, alog_ref,
                dskip_ref, ng_ref, expand_ref, expand_t_ref, shift_ref,
                y_ref, cnew_ref, hlast_ref,
                h_s, tail_s, xc_s, dte_s, ece_s, wee_s,
                *, chunk, t_valid, n_groups, heads_per_group, head_dim, d_state):
    L = chunk
    G, R, P, N = n_groups, heads_per_group, head_dim, d_state
    GW = R * P
    d_inner = G * GW
    c = pl.program_id(1)
    nc = pl.num_programs(1)

    TAIL = slice(XWIN_PAD - (CONV_W - 1), XWIN_PAD)

    @pl.when(c == 0)
    def _():
        h_s[...] = hprev_ref[...]
        tail_s[...] = jnp.zeros_like(tail_s)
        tail_s[TAIL, :] = cprev_ref[...]

    fix = tail_s[TAIL.start:TAIL.start + XWIN_PAD, :] * cw_ref[0:1, :]
    for kk in range(1, CONV_W - 1):
        fix = fix + tail_s[TAIL.start + kk:TAIL.start + kk + XWIN_PAD, :] * cw_ref[kk:kk + 1, :]
    conv_dim = xbc_ref.shape[1]
    for ct in range(conv_dim // CONV_TILE):
        cs = slice(ct * CONV_TILE, (ct + 1) * CONV_TILE)
        xb = xbc_ref[:, cs]
        sh = _dot(shift_ref[...], xb)
        pre = xb.astype(F32) * cw_ref[CONV_W - 1:CONV_W, cs] + cb_ref[:, cs]
        for kk in range(CONV_W - 1):
            pre = pre + sh[kk * L:(kk + 1) * L] * cw_ref[kk:kk + 1, cs]
        xc_s[0:XWIN_PAD, cs] = _silu(pre[0:XWIN_PAD] + fix[:, cs])
        xc_s[XWIN_PAD:L, cs] = _silu(pre[XWIN_PAD:L])

    dt = _softplus(dtr_ref[...] + dtb_ref[...])
    if t_valid % L:
        row = lax.broadcasted_iota(jnp.int32, (L, LANES), 0) + c * L
        dt = jnp.where(row < t_valid, dt, 0.0)
    a_neg = -jnp.exp(alog_ref[...])
    ri = lax.broadcasted_iota(jnp.int32, (L, L), 0)
    ci = lax.broadcasted_iota(jnp.int32, (L, L), 1)
    causal = ci <= ri
    tri = causal.astype(BF16)
    cum = _dot_01_by_f32(tri, dt * a_neg)
    cum_t = cum.T
    cum_last = cum[L - 1:L, :]
    per_head = (dt, jnp.exp(cum), jnp.exp(cum_last - cum) * dt)
    stacked = jnp.concatenate([p for a in per_head for p in _split3(a)[:2]], axis=0)
    for ct in range(d_inner // CONV_TILE):
        cs = slice(ct * CONV_TILE, (ct + 1) * CONV_TILE)
        ex = _dot(stacked, expand_ref[:, cs])
        for dst, j in ((dte_s, 0), (ece_s, 1), (wee_s, 2)):
            dst[:, cs] = ex[2 * j * L:(2 * j + 1) * L] + ex[(2 * j + 1) * L:(2 * j + 2) * L]
    cum_last_cols = jnp.broadcast_to(cum_t[:, L - 1:L], (LANES, N))
    dec_rows = jnp.exp(_dot_01_by_f32(expand_t_ref[...], cum_last_cols))

    lane = lax.broadcasted_iota(jnp.int32, (L, 2 * P), 1)
    lo_half = lane < P

    def group_matmuls(g):
        gs = slice(g * GW, (g + 1) * GW)
        b_g = xc_s[:, d_inner + g * N:d_inner + (g + 1) * N].astype(BF16)
        c_g = xc_s[:, d_inner + G * N + g * N:d_inner + G * N + (g + 1) * N].astype(BF16)
        x_g = xc_s[:, gs]
        cb = jnp.where(causal, _dot_nt(c_g, b_g), 0.0)
        h_g = h_s[gs, :]
        y_off = _dot_nt(c_g, h_g.astype(BF16)) * ece_s[:, gs]
        xw_t = (x_g * wee_s[:, gs]).T.astype(BF16)
        h_s[gs, :] = h_g * dec_rows[gs, :] + _dot(xw_t, b_g)
        return gs, x_g, cb, y_off

    def group_intra(g, gs, x_g, cb, y_off):
        xdt = (x_g * dte_s[:, gs]).astype(BF16)
        y_parts = []
        for pr in range(R // 2):
            ms = []
            for r in (2 * pr, 2 * pr + 1):
                hh = g * R + r
                seg = cum[:, hh:hh + 1] - cum_t[hh:hh + 1, :]
                ms.append((cb * jnp.exp(jnp.minimum(seg, 0.0))).astype(BF16))
            m2 = jnp.concatenate(ms, axis=1)
            xp = xdt[:, pr * 2 * P:(pr + 1) * 2 * P]
            zero = jnp.zeros_like(xp)
            w2 = jnp.concatenate([jnp.where(lo_half, xp, zero), jnp.where(lo_half, zero, xp)], axis=0)
            y_parts.append(_dot(m2, w2))
        return jnp.concatenate(y_parts, axis=1) + y_off + dskip_ref[:, gs] * x_g

    def group_finish(gs, y):
        zg = z_ref[:, gs].astype(F32)
        y = y * _silu(zg)
        y = y * lax.rsqrt(jnp.mean(y * y, axis=-1, keepdims=True) + RMS_EPS)
        y_ref[:, gs] = (y * ng_ref[:, gs]).astype(y_ref.dtype)

    for g0 in range(0, G, GROUP_BATCH):
        batch = range(g0, min(g0 + GROUP_BATCH, G))
        staged = [group_matmuls(g) for g in batch]
        ys = [group_intra(g, *st) for g, st in zip(batch, staged)]
        for st, y in zip(staged, ys):
            group_finish(st[0], y)

    last_row = (t_valid - 1) % L + 1
    assert last_row % BF16_ROWS == 0
    last_rows = slice(BF16_ROWS - (CONV_W - 1), BF16_ROWS)

    @pl.when(c < nc - 1)
    def _():
        tail_s[TAIL, :] = xbc_ref[L - BF16_ROWS:L, :].astype(F32)[last_rows]

    @pl.when(c == nc - 1)
    def _():
        cnew_ref[...] = xbc_ref[last_row - BF16_ROWS:last_row, :].astype(F32)[last_rows]
        hlast_ref[...] = h_s[...]


def _ssd(xbc, z, dtr, conv_prev, h_prev, cw, cb, dtb, alog, dskip_e, ng, expand, expand_t, shift, t_valid):
    bsz, t_pad, conv_dim = xbc.shape
    d_inner = z.shape[2]
    L = SSD_CHUNK
    nc = t_pad // L
    n_heads = d_inner // SSM_HEAD_DIM
    kern = functools.partial(_ssd_kernel, chunk=L, t_valid=t_valid, n_groups=SSM_GROUPS,
                             heads_per_group=n_heads // SSM_GROUPS, head_dim=SSM_HEAD_DIM, d_state=D_STATE)
    tok = lambda w: pl.BlockSpec((None, L, w), lambda b, c: (b, c, 0))
    per_b = lambda s: pl.BlockSpec((None,) + s, lambda b, c: (b,) + (0,) * len(s))
    return pl.pallas_call(
        kern,
        grid=(bsz, nc),
        in_specs=[tok(conv_dim), tok(d_inner), tok(LANES),
                  per_b((CONV_W - 1, conv_dim)), per_b((d_inner, D_STATE)),
                  _const_spec(cw.shape), _const_spec(cb.shape), _const_spec(dtb.shape), _const_spec(alog.shape),
                  _const_spec(dskip_e.shape), _const_spec(ng.shape), _const_spec(expand.shape),
                  _const_spec(expand_t.shape), _const_spec(shift.shape)],
        out_specs=[tok(d_inner), per_b((CONV_W - 1, conv_dim)), per_b((d_inner, D_STATE))],
        out_shape=[jax.ShapeDtypeStruct((bsz, t_pad, d_inner), BF16),
                   jax.ShapeDtypeStruct((bsz, CONV_W - 1, conv_dim), F32),
                   jax.ShapeDtypeStruct((bsz, d_inner, D_STATE), F32)],
        scratch_shapes=[pltpu.VMEM((d_inner, D_STATE), F32),
                        pltpu.VMEM((2 * XWIN_PAD, conv_dim), F32),
                        pltpu.VMEM((L, conv_dim), F32),
                        pltpu.VMEM((L, d_inner), F32),
                        pltpu.VMEM((L, d_inner), F32),
                        pltpu.VMEM((L, d_inner), F32)],
        compiler_params=pltpu.CompilerParams(dimension_semantics=("parallel", "arbitrary"),
                                             vmem_limit_bytes=VMEM_LIMIT),
        name="ssd",
    )(xbc, z, dtr, conv_prev, h_prev, cw, cb, dtb, alog, dskip_e, ng, expand, expand_t, shift)


def _post_kernel(*refs, ff_chunk, with_kv, row_groups):
    if with_kv:
        (x_ref, m_ref, p_ref, wo_ref, g1_ref, b1_ref, w1_ref, w2_ref, g2_ref, b2_ref, wg_ref, wp_ref,
         kg_ref, kb_ref, wkv_ref, o_ref, k_ref, v_ref, k16_ref, v16_ref) = refs
    else:
        (x_ref, m_ref, p_ref, wo_ref, g1_ref, b1_ref, w1_ref, w2_ref, g2_ref, b2_ref, wg_ref, wp_ref,
         o_ref) = refs
    tm = x_ref.shape[0]
    rows = [slice(s * (tm // row_groups), (s + 1) * (tm // row_groups)) for s in range(row_groups)]
    d_ff = w1_ref.shape[1]

    def ln1(r):
        return _layer_norm(ALPHA * x_ref[r, :] + _dot(m_ref[r, :], wo_ref[...]), g1_ref[...], b1_ref[...])

    def mlp(x1):
        x1b = x1.astype(BF16)
        acc = ALPHA * x1
        for cc in range(d_ff // ff_chunk):
            cs = slice(cc * ff_chunk, (cc + 1) * ff_chunk)
            h = jnp.square(jnp.maximum(_dot(x1b, w1_ref[:, cs]), 0.0)).astype(BF16)
            acc = acc + _dot(h, w2_ref[cs, :])
        return acc

    def finish(pre2, r):
        x2 = _layer_norm(pre2, g2_ref[...], b2_ref[...])
        gate = jax.nn.sigmoid(_dot(x2.astype(BF16), wg_ref[...]))
        pe = _dot(p_ref[r, :].astype(BF16), wp_ref[...])
        x3 = x2 + gate * pe
        o_ref[r, :] = x3
        if with_kv:
            kvn = _layer_norm(x3, kg_ref[...], kb_ref[...]).astype(BF16)
            kv = _dot(kvn, wkv_ref[...])
            w = k_ref.shape[1]
            k_ref[r, :] = kv[:, :w]
            v_ref[r, :] = kv[:, w:]
            k16_ref[r, :] = kv[:, :w].astype(BF16)
            v16_ref[r, :] = kv[:, w:].astype(BF16)

    x1 = [ln1(rows[0])]
    pre2 = []
    for s in range(row_groups):
        if s + 1 < row_groups:
            x1.append(ln1(rows[s + 1]))
        pre2.append(mlp(x1[s]))
        if s > 0:
            finish(pre2[s - 1], rows[s - 1])
    finish(pre2[-1], rows[-1])


def _post(xf, mf, pf, wo, g1, b1, w1, w2, g2, b2, wg, wp, kv_params, tm):
    n, d = xf.shape
    km, pd = mf.shape[1], pf.shape[1]
    with_kv = kv_params is not None
    row = lambda w: pl.BlockSpec((tm, w), lambda i: (i, 0))
    ins = [xf, mf, pf, wo, g1, b1, w1, w2, g2, b2, wg, wp]
    in_specs = [row(d), row(km), row(pd)] + [_const_spec(a.shape) for a in ins[3:]]
    out_specs = [row(d)]
    out_shape = [jax.ShapeDtypeStruct((n, d), F32)]
    if with_kv:
        ins += list(kv_params)
        in_specs += [_const_spec(a.shape) for a in kv_params]
        w = kv_params[2].shape[1] // 2
        out_specs += [row(w)] * 4
        out_shape += [jax.ShapeDtypeStruct((n, w), F32)] * 2 + [jax.ShapeDtypeStruct((n, w), BF16)] * 2
    return pl.pallas_call(
        functools.partial(_post_kernel, ff_chunk=min(FF_CHUNK, w1.shape[1]), with_kv=with_kv,
                          row_groups=1 if with_kv else ROW_GROUPS),
        grid=(n // tm,),
        in_specs=in_specs, out_specs=out_specs, out_shape=out_shape,
        compiler_params=pltpu.CompilerParams(dimension_semantics=("parallel",),
                                             vmem_limit_bytes=VMEM_LIMIT),
        name="post_kv" if with_kv else "post",
    )(*ins)


def _attn_kernel(x_ref, wq_ref, k_ref, v_ref, ut_ref, o_ref, q_s, suf_s, acc_s,
                 *, tq, q_offset, n_heads, head_dim, scale):
    KB = SB_BLOCK
    qi = pl.program_id(1)
    q = _dot(x_ref[...].astype(BF16), wq_ref[...]) * (scale * LOG2E)
    q_s[...] = q.astype(BF16)
    suf_s[...] = jnp.zeros_like(suf_s)
    acc_s[...] = jnp.zeros_like(acc_s)
    q_min = q_offset + qi * tq
    n_full = q_min // KB
    n_diag = -(-tq // KB)
    ut = ut_ref[...]
    heads = [slice(h * head_dim, (h + 1) * head_dim) for h in range(n_heads)]

    def key_block(kstart, masked, row_lo=0):
        rows = slice(row_lo, tq)
        if masked:
            qpos = q_min + row_lo + lax.broadcasted_iota(jnp.int32, (tq - row_lo, KB), 0)
            kpos = kstart + lax.broadcasted_iota(jnp.int32, (tq - row_lo, KB), 1)
            mask = kpos < qpos
        z2 = [_dot_nt(q_s[rows, hs], k_ref[pl.ds(kstart, KB), hs]) for hs in heads]
        sp2, lsig2 = [], []
        for z in z2:
            neg_abs = lax.bitcast_convert_type(lax.bitcast_convert_type(z, jnp.int32) | SIGN_BIT, F32)
            sp = jnp.maximum(z, 0.0) + jnp.log(1.0 + jnp.exp2(neg_abs)) * LOG2E
            lsig2.append(z - sp)
            if masked:
                sp = jnp.where(mask, sp, 0.0)
            sp2.append(sp.astype(BF16))
        it = [_dot(sp, ut) for sp in sp2]
        w = []
        for h in range(n_heads):
            wh = jnp.exp2(lsig2[h] - it[h][:, :KB] - suf_s[h, rows, :])
            if masked:
                wh = jnp.where(mask, wh, 0.0)
            w.append(wh.astype(BF16))
            suf_s[h, rows, :] = suf_s[h, rows, :] + it[h][:, KB:]
        for h, hs in enumerate(heads):
            acc_s[rows, hs] = acc_s[rows, hs] + _dot(w[h], v_ref[pl.ds(kstart, KB), hs])

    def any_row_alive():
        return (jnp.min(suf_s[...]) < UNDERFLOW_LOG2).astype(jnp.int32)

    for d in reversed(range(n_diag)):
        key_block(pl.multiple_of(q_min + d * KB, KB), True, row_lo=d * KB)

    def cond(carry):
        i, alive = carry
        return jnp.logical_and(i < n_full, alive > 0)

    def body(carry):
        i, _ = carry
        key_block(pl.multiple_of((n_full - 1 - i) * KB, KB), False)
        return i + 1, any_row_alive()

    lax.while_loop(cond, body, (jnp.int32(0), jnp.int32(1)))
    o_ref[...] = acc_s[...].astype(o_ref.dtype)


def _attn(x, wq, k16, v16, ut, q_offset, tq):
    bsz, t, d = x.shape
    tk, width = k16.shape[1], k16.shape[2]
    n_heads = width // SB_HEAD_DIM
    assert q_offset % SB_BLOCK == 0 and t % tq == 0 and (tq % SB_BLOCK == 0 or t == tq)
    assert tk % SB_BLOCK == 0 and tk >= q_offset + (-(-t // SB_BLOCK)) * SB_BLOCK
    kern = functools.partial(_attn_kernel, tq=tq, q_offset=q_offset, n_heads=n_heads,
                             head_dim=SB_HEAD_DIM, scale=SB_HEAD_DIM ** -0.5)
    return pl.pallas_call(
        kern,
        grid=(bsz, t // tq),
        in_specs=[pl.BlockSpec((None, tq, d), lambda b, i: (b, i, 0)),
                  _const_spec(wq.shape),
                  pl.BlockSpec((None, tk, width), lambda b, i: (b, 0, 0)),
                  pl.BlockSpec((None, tk, width), lambda b, i: (b, 0, 0)),
                  _const_spec(ut.shape)],
        out_specs=pl.BlockSpec((None, tq, width), lambda b, i: (b, i, 0)),
        out_shape=jax.ShapeDtypeStruct((bsz, t, width), BF16),
        scratch_shapes=[pltpu.VMEM((tq, width), BF16),
                        pltpu.VMEM((n_heads, tq, SB_BLOCK), F32),
                        pltpu.VMEM((tq, width), F32)],
        compiler_params=pltpu.CompilerParams(dimension_semantics=("parallel", "arbitrary"),
                                             vmem_limit_bytes=VMEM_LIMIT),
        name="sb_attn",
    )(x, wq, k16, v16, ut)


def _row(v):
    return v.reshape(1, -1).astype(F32)


def _pad_lanes(v):
    return jnp.pad(v.reshape(1, -1).astype(F32), ((0, 0), (0, LANES - v.size)))


def _prep_weights(prm):
    d_inner = prm['a_w_out'].shape[1]
    conv_dim = prm['a_conv_w'].shape[2]
    n_heads = prm['a_dt_bias'].shape[1]
    w = dict(prm)
    w['wz'] = prm['a_w_in'][:, :, :d_inner].astype(BF16)
    w['wx'] = prm['a_w_in'][:, :, d_inner:d_inner + conv_dim].astype(BF16)
    w['wd'] = jnp.pad(prm['a_w_in'][:, :, d_inner + conv_dim:], ((0, 0), (0, 0), (0, LANES - n_heads))).astype(BF16)
    for name in ('a_w_out', 'w_kv', 'b_w_q', 'b_w_out', 'mlp_w1', 'mlp_w2', 'ple_w', 'ple_gate_w'):
        w[name] = prm[name].astype(BF16)
    head_of_channel = jnp.arange(d_inner) // SSM_HEAD_DIM
    w['expand'] = (jnp.arange(LANES)[:, None] == head_of_channel[None, :]).astype(BF16)
    tt = jnp.arange(SSD_CHUNK)
    w['shift'] = jnp.concatenate([(tt[None, :] == tt[:, None] - s).astype(BF16) for s in range(CONV_W - 1, 0, -1)], axis=0)
    idx = jnp.arange(SB_BLOCK)
    upper = (idx[:, None] > idx[None, :]).astype(BF16)
    w['ut'] = jnp.concatenate([upper, jnp.ones((SB_BLOCK, SB_BLOCK), BF16)], axis=1)
    return w


def _run_trunk(x, p, conv_prev, ssm_prev, k_past, v_past, q_offset, w, tm, tq):
    bsz, t, d = x.shape
    n = bsz * t
    d_inner = w['a_w_out'].shape[1]
    xf = x.reshape(n, d)
    conv_states, ssm_states = [], []
    k_new = v_new = k16 = v16 = None
    t_pad = -(-t // SSD_CHUNK) * SSD_CHUNK
    for i in range(DEPTH):
        if i < N_A:
            z, xbc, dtr = _inproj(xf, w['wz'][i], w['wx'][i], w['wd'][i], min(INPROJ_ROWS, n))
            z, xbc, dtr = (a.reshape(bsz, t, -1) for a in (z, xbc, dtr))
            if t_pad != t:
                z, xbc, dtr = (jnp.pad(a, ((0, 0), (0, t_pad - t), (0, 0))) for a in (z, xbc, dtr))
            y, cnew, hlast = _ssd(
                xbc, z, dtr, conv_prev[i], ssm_prev[i].reshape(bsz, d_inner, D_STATE),
                w['a_conv_w'][i], _row(w['a_conv_b'][i]), _pad_lanes(w['a_dt_bias'][i]),
                _pad_lanes(w['a_A_log'][i]), _row(jnp.repeat(w['a_d_skip'][i], SSM_HEAD_DIM)),
                _row(w['a_norm_g'][i]), w['expand'], w['expand'].T, w['shift'], t)
            conv_states.append(cnew)
            ssm_states.append(hlast.reshape(bsz, d_inner // SSM_HEAD_DIM, SSM_HEAD_DIM, D_STATE))
            mf = y[:, :t].reshape(n, d_inner)
            wo = w['a_w_out'][i]
        else:
            j = i - N_A
            o = _attn(xf.reshape(bsz, t, d), w['b_w_q'][j], k16, v16, w['ut'], q_offset, tq)
            mf = o.reshape(n, -1)
            wo = w['b_w_out'][j]
        kv_params = (_row(w['kv_norm_g']), _row(w['kv_norm_b']), w['w_kv']) if i == N_A - 1 else None
        outs = _post(xf, mf, p[i].reshape(n, -1), wo, _row(w['ln1_g'][i]), _row(w['ln1_b'][i]),
                     w['mlp_w1'][i], w['mlp_w2'][i], _row(w['ln2_g'][i]), _row(w['ln2_b'][i]),
                     w['ple_gate_w'][i], w['ple_w'][i], kv_params, tm)
        xf = outs[0]
        if kv_params is not None:
            k_new, v_new, kb, vb = outs[1:]
            width = kb.shape[1]
            k16, v16 = kb.reshape(bsz, t, width), vb.reshape(bsz, t, width)
            if k_past is not None:
                k16 = jnp.concatenate([k_past.reshape(bsz, -1, width).astype(BF16), k16], axis=1)
                v16 = jnp.concatenate([v_past.reshape(bsz, -1, width).astype(BF16), v16], axis=1)
            tk = k16.shape[1]
            tk_pad = -(-tk // SB_BLOCK) * SB_BLOCK
            if tk_pad != tk:
                k16 = jnp.pad(k16, ((0, 0), (0, tk_pad - tk), (0, 0)))
                v16 = jnp.pad(v16, ((0, 0), (0, tk_pad - tk), (0, 0)))
    n_sb = k_new.shape[1] // SB_HEAD_DIM
    return (xf.reshape(bsz, t, d), jnp.stack(ssm_states), jnp.stack(conv_states),
            k_new.reshape(bsz, t, n_sb, SB_HEAD_DIM), v_new.reshape(bsz, t, n_sb, SB_HEAD_DIM))


def kernel(x_prompt, x_sample, p_prompt, p_sample, state_ssm, state_conv, cache_k, cache_v, a_w_in, a_conv_w, a_conv_b, a_dt_bias, a_A_log, a_d_skip, a_norm_g, a_w_out, kv_norm_g, kv_norm_b, w_kv, b_w_q, b_w_out, ln1_g, ln1_b, ln2_g, ln2_b, mlp_w1, mlp_w2, ple_w, ple_gate_w):
    prm = {
        'a_w_in': a_w_in, 'a_conv_w': a_conv_w, 'a_conv_b': a_conv_b, 'a_dt_bias': a_dt_bias,
        'a_A_log': a_A_log, 'a_d_skip': a_d_skip, 'a_norm_g': a_norm_g, 'a_w_out': a_w_out,
        'kv_norm_g': kv_norm_g, 'kv_norm_b': kv_norm_b, 'w_kv': w_kv,
        'b_w_q': b_w_q, 'b_w_out': b_w_out,
        'ln1_g': ln1_g, 'ln1_b': ln1_b, 'ln2_g': ln2_g, 'ln2_b': ln2_b,
        'mlp_w1': mlp_w1, 'mlp_w2': mlp_w2, 'ple_w': ple_w, 'ple_gate_w': ple_gate_w,
    }
    w = _prep_weights(prm)
    bp, tp, _ = x_prompt.shape
    bs, ts, _ = x_sample.shape
    n_heads = a_dt_bias.shape[1]
    conv0 = jnp.zeros((N_A, bp, CONV_W - 1, a_conv_w.shape[2]), F32)
    ssm0 = jnp.zeros((N_A, bp, n_heads, SSM_HEAD_DIM, D_STATE), F32)
    y_p, ssm_p, conv_p, k_p, v_p = _run_trunk(x_prompt, p_prompt, conv0, ssm0, None, None, 0, w,
                                              tm=min(512, bp * tp), tq=min(256, tp))
    y_s, ssm_s, conv_s, k_s, v_s = _run_trunk(x_sample, p_sample, state_conv, state_ssm, cache_k, cache_v,
                                              cache_k.shape[1], w, tm=min(512, bs * ts), tq=ts)
    return (y_p, y_s, ssm_p, conv_p, k_p, v_p, ssm_s, conv_s, k_s, v_s)
```

```python
import functools

import jax
import jax.numpy as jnp
from jax import lax
from jax.experimental import pallas as pl
from jax.experimental.pallas import tpu as pltpu

F32 = jnp.float32
BF16 = jnp.bfloat16

DEPTH = 4
N_A = DEPTH // 2
SSM_HEAD_DIM = 64
SSM_GROUPS = 8
D_STATE = 128
CONV_W = 4
SB_HEAD_DIM = 128
SB_BLOCK = 128
ALPHA = (2.0 * DEPTH) ** 0.25
LN_EPS = 1e-5
RMS_EPS = 1e-5

LANES = 128
SSD_CHUNK = 128
XWIN_PAD = 8
BF16_ROWS = 16
CONV_TILE = 256
GROUP_BATCH = 2
INPROJ_ROWS = 1024
ROW_GROUPS = 2
FF_CHUNK = 1024
VMEM_LIMIT = 56 * 1024 * 1024
LOG2E = 1.4426950408889634
SIGN_BIT = -2147483648
UNDERFLOW_LOG2 = 136.0


def _const_spec(shape):
    nd = len(shape)
    return pl.BlockSpec(shape, lambda *_: (0,) * nd, pipeline_mode=pl.Buffered(1))


def _dot(a, b):
    return jnp.dot(a, b, preferred_element_type=F32)


def _dot_nt(a, b):
    return lax.dot_general(a, b, (((1,), (1,)), ((), ())), preferred_element_type=F32)


def _split3(a):
    a1 = a.astype(BF16)
    r1 = a - a1.astype(F32)
    a2 = r1.astype(BF16)
    a3 = (r1 - a2.astype(F32)).astype(BF16)
    return a1, a2, a3


def _dot_f32_by01(a, m01, pieces=3):
    return sum(_dot(p, m01) for p in _split3(a)[:pieces])


def _dot_01_by_f32(m01, a):
    a1, a2, a3 = _split3(a)
    return _dot(m01, a1) + _dot(m01, a2) + _dot(m01, a3)


def _layer_norm(x, g, b):
    mu = jnp.mean(x, axis=-1, keepdims=True)
    xc = x - mu
    var = jnp.mean(xc * xc, axis=-1, keepdims=True)
    return xc * lax.rsqrt(var + LN_EPS) * g + b


def _silu(x):
    return x * jax.nn.sigmoid(x)


def _softplus(x):
    return jnp.maximum(x, 0.0) + jnp.log(1.0 + jnp.exp(-jnp.abs(x)))


def _inproj_kernel(x_ref, wz_ref, wx_ref, wd_ref, z_ref, xbc_ref, dt_ref):
    xb = x_ref[...].astype(BF16)
    z_ref[...] = _dot(xb, wz_ref[...]).astype(z_ref.dtype)
    xbc_ref[...] = _dot(xb, wx_ref[...]).astype(xbc_ref.dtype)
    dt_ref[...] = _dot(xb, wd_ref[...])


def _inproj(xf, wz, wx, wd, tm):
    n, d = xf.shape
    d_inner, conv_dim = wz.shape[1], wx.shape[1]
    return pl.pallas_call(
        _inproj_kernel,
        grid=(n // tm,),
        in_specs=[pl.BlockSpec((tm, d), lambda i: (i, 0)),
                  _const_spec(wz.shape), _const_spec(wx.shape), _const_spec(wd.shape)],
        out_specs=[pl.BlockSpec((tm, d_inner), lambda i: (i, 0)),
                   pl.BlockSpec((tm, conv_dim), lambda i: (i, 0)),
                   pl.BlockSpec((tm, LANES), lambda i: (i, 0))],
        out_shape=[jax.ShapeDtypeStruct((n, d_inner), BF16),
                   jax.ShapeDtypeStruct((n, conv_dim), BF16),
                   jax.ShapeDtypeStruct((n, LANES), F32)],
        compiler_params=pltpu.CompilerParams(dimension_semantics=("parallel",),
                                             vmem_limit_bytes=VMEM_LIMIT),
        name="inproj",
    )(xf, wz, wx, wd)


def _ssd_kernel(xbc_ref, z_ref, dtr_ref, cprev_ref, hprev_ref, cw_ref, cb_ref, dtb_ref, alog_ref,
                dskip_ref, ng_ref, expand_ref, expand_t_ref, shift_ref,
                y_ref, cnew_ref, hlast_ref,
                h_s, tail_s, xc_s, dte_s, ece_s, wee_s,
                *, chunk, t_valid, n_groups, heads_per_group, head_dim, d_state):
    L = chunk
    G, R, P, N = n_groups, heads_per_group, head_dim, d_state
    GW = R * P
    d_inner = G * GW
    c = pl.program_id(1)
    nc = pl.num_programs(1)

    TAIL = slice(XWIN_PAD - (CONV_W - 1), XWIN_PAD)

    @pl.when(c == 0)
    def _():
        h_s[...] = hprev_ref[...]
        tail_s[...] = jnp.zeros_like(tail_s)
        tail_s[TAIL, :] = cprev_ref[...]

    fix = tail_s[TAIL.start:TAIL.start + XWIN_PAD, :] * cw_ref[0:1, :]
    for kk in range(1, CONV_W - 1):
        fix = fix + tail_s[TAIL.start + kk:TAIL.start + kk + XWIN_PAD, :] * cw_ref[kk:kk + 1, :]
    conv_dim = xbc_ref.shape[1]
    for ct in range(conv_dim // CONV_TILE):
        cs = slice(ct * CONV_TILE, (ct + 1) * CONV_TILE)
        xb = xbc_ref[:, cs]
        sh = _dot(shift_ref[...], xb)
        pre = xb.astype(F32) * cw_ref[CONV_W - 1:CONV_W, cs] + cb_ref[:, cs]
        for kk in range(CONV_W - 1):
            pre = pre + sh[kk * L:(kk + 1) * L] * cw_ref[kk:kk + 1, cs]
        xc_s[0:XWIN_PAD, cs] = _silu(pre[0:XWIN_PAD] + fix[:, cs])
        xc_s[XWIN_PAD:L, cs] = _silu(pre[XWIN_PAD:L])

    dt = _softplus(dtr_ref[...] + dtb_ref[...])
    if t_valid % L:
        row = lax.broadcasted_iota(jnp.int32, (L, LANES), 0) + c * L
        dt = jnp.where(row < t_valid, dt, 0.0)
    a_neg = -jnp.exp(alog_ref[...])
    ri = lax.broadcasted_iota(jnp.int32, (L, L), 0)
    ci = lax.broadcasted_iota(jnp.int32, (L, L), 1)
    causal = ci <= ri
    tri = causal.astype(BF16)
    cum = _dot_01_by_f32(tri, dt * a_neg)
    cum_t = cum.T
    cum_last = cum[L - 1:L, :]
    per_head = (dt, jnp.exp(cum), jnp.exp(cum_last - cum) * dt)
    stacked = jnp.concatenate([p for a in per_head for p in _split3(a)[:2]], axis=0)
    for ct in range(d_inner // CONV_TILE):
        cs = slice(ct * CONV_TILE, (ct + 1) * CONV_TILE)
        ex = _dot(stacked, expand_ref[:, cs])
        for dst, j in ((dte_s, 0), (ece_s, 1), (wee_s, 2)):
            dst[:, cs] = ex[2 * j * L:(2 * j + 1) * L] + ex[(2 * j + 1) * L:(2 * j + 2) * L]
    cum_last_cols = jnp.broadcast_to(cum_t[:, L - 1:L], (LANES, N))
    dec_rows = jnp.exp(_dot_01_by_f32(expand_t_ref[...], cum_last_cols))

    lane = lax.broadcasted_iota(jnp.int32, (L, 2 * P), 1)
    lo_half = lane < P

    def group_matmuls(g):
        gs = slice(g * GW, (g + 1) * GW)
        b_g = xc_s[:, d_inner + g * N:d_inner + (g + 1) * N].astype(BF16)
        c_g = xc_s[:, d_inner + G * N + g * N:d_inner + G * N + (g + 1) * N].astype(BF16)
        x_g = xc_s[:, gs]
        cb = jnp.where(causal, _dot_nt(c_g, b_g), 0.0)
        h_g = h_s[gs, :]
        y_off = _dot_nt(c_g, h_g.astype(BF16)) * ece_s[:, gs]
        xw_t = (x_g * wee_s[:, gs]).T.astype(BF16)
        h_s[gs, :] = h_g * dec_rows[gs, :] + _dot(xw_t, b_g)
        return gs, x_g, cb, y_off

    def group_intra(g, gs, x_g, cb, y_off):
        xdt = (x_g * dte_s[:, gs]).astype(BF16)
        y_parts = []
        for pr in range(R // 2):
            ms = []
            for r in (2 * pr, 2 * pr + 1):
                hh = g * R + r
                seg = cum[:, hh:hh + 1] - cum_t[hh:hh + 1, :]
                ms.append((cb * jnp.exp(jnp.minimum(seg, 0.0))).astype(BF16))
            m2 = jnp.concatenate(ms, axis=1)
            xp = xdt[:, pr * 2 * P:(pr + 1) * 2 * P]
            zero = jnp.zeros_like(xp)
            w2 = jnp.concatenate([jnp.where(lo_half, xp, zero), jnp.where(lo_half, zero, xp)], axis=0)
            y_parts.append(_dot(m2, w2))
        return jnp.concatenate(y_parts, axis=1) + y_off + dskip_ref[:, gs] * x_g

    def group_finish(gs, y):
        zg = z_ref[:, gs].astype(F32)
        y = y * _silu(zg)
        y = y * lax.rsqrt(jnp.mean(y * y, axis=-1, keepdims=True) + RMS_EPS)
        y_ref[:, gs] = (y * ng_ref[:, gs]).astype(y_ref.dtype)

    for g0 in range(0, G, GROUP_BATCH):
        batch = range(g0, min(g0 + GROUP_BATCH, G))
        staged = [group_matmuls(g) for g in batch]
        ys = [group_intra(g, *st) for g, st in zip(batch, staged)]
        for st, y in zip(staged, ys):
            group_finish(st[0], y)

    last_row = (t_valid - 1) % L + 1
    assert last_row % BF16_ROWS == 0
    last_rows = slice(BF16_ROWS - (CONV_W - 1), BF16_ROWS)

    @pl.when(c < nc - 1)
    def _():
        tail_s[TAIL, :] = xbc_ref[L - BF16_ROWS:L, :].astype(F32)[last_rows]

    @pl.when(c == nc - 1)
    def _():
        cnew_ref[...] = xbc_ref[last_row - BF16_ROWS:last_row, :].astype(F32)[last_rows]
        hlast_ref[...] = h_s[...]


def _ssd(xbc, z, dtr, conv_prev, h_prev, cw, cb, dtb, alog, dskip_e, ng, expand, expand_t, shift, t_valid):
    bsz, t_pad, conv_dim = xbc.shape
    d_inner = z.shape[2]
    L = SSD_CHUNK
    nc = t_pad // L
    n_heads = d_inner // SSM_HEAD_DIM
    kern = functools.partial(_ssd_kernel, chunk=L, t_valid=t_valid, n_groups=SSM_GROUPS,
                             heads_per_group=n_heads // SSM_GROUPS, head_dim=SSM_HEAD_DIM, d_state=D_STATE)
    tok = lambda w: pl.BlockSpec((None, L, w), lambda b, c: (b, c, 0))
    per_b = lambda s: pl.BlockSpec((None,) + s, lambda b, c: (b,) + (0,) * len(s))
    return pl.pallas_call(
        kern,
        grid=(bsz, nc),
        in_specs=[tok(conv_dim), tok(d_inner), tok(LANES),
                  per_b((CONV_W - 1, conv_dim)), per_b((d_inner, D_STATE)),
                  _const_spec(cw.shape), _const_spec(cb.shape), _const_spec(dtb.shape), _const_spec(alog.shape),
                  _const_spec(dskip_e.shape), _const_spec(ng.shape), _const_spec(expand.shape),
                  _const_spec(expand_t.shape), _const_spec(shift.shape)],
        out_specs=[tok(d_inner), per_b((CONV_W - 1, conv_dim)), per_b((d_inner, D_STATE))],
        out_shape=[jax.ShapeDtypeStruct((bsz, t_pad, d_inner), BF16),
                   jax.ShapeDtypeStruct((bsz, CONV_W - 1, conv_dim), F32),
                   jax.ShapeDtypeStruct((bsz, d_inner, D_STATE), F32)],
        scratch_shapes=[pltpu.VMEM((d_inner, D_STATE), F32),
                        pltpu.VMEM((2 * XWIN_PAD, conv_dim), F32),
                        pltpu.VMEM((L, conv_dim), F32),
                        pltpu.VMEM((L, d_inner), F32),
                        pltpu.VMEM((L, d_inner), F32),
                        pltpu.VMEM((L, d_inner), F32)],
        compiler_params=pltpu.CompilerParams(dimension_semantics=("parallel", "arbitrary"),
                                             vmem_limit_bytes=VMEM_LIMIT),
        name="ssd",
    )(xbc, z, dtr, conv_prev, h_prev, cw, cb, dtb, alog, dskip_e, ng, expand, expand_t, shift)


def _post_kernel(*refs, ff_chunk, with_kv, row_groups):
    if with_kv:
        (x_ref, m_ref, p_ref, wo_ref, g1_ref, b1_ref, w1_ref, w2_ref, g2_ref, b2_ref, wg_ref, wp_ref,
         kg_ref, kb_ref, wkv_ref, o_ref, k_ref, v_ref, k16_ref, v16_ref) = refs
    else:
        (x_ref, m_ref, p_ref, wo_ref, g1_ref, b1_ref, w1_ref, w2_ref, g2_ref, b2_ref, wg_ref, wp_ref,
         o_ref) = refs
    tm = x_ref.shape[0]
    rows = [slice(s * (tm // row_groups), (s + 1) * (tm // row_groups)) for s in range(row_groups)]
    d_ff = w1_ref.shape[1]

    def ln1(r):
        return _layer_norm(ALPHA * x_ref[r, :] + _dot(m_ref[r, :], wo_ref[...]), g1_ref[...], b1_ref[...])

    def mlp(x1):
        x1b = x1.astype(BF16)
        acc = ALPHA * x1
        for cc in range(d_ff // ff_chunk):
            cs = slice(cc * ff_chunk, (cc + 1) * ff_chunk)
            h = jnp.square(jnp.maximum(_dot(x1b, w1_ref[:, cs]), 0.0)).astype(BF16)
            acc = acc + _dot(h, w2_ref[cs, :])
        return acc

    def finish(pre2, r):
        x2 = _layer_norm(pre2, g2_ref[...], b2_ref[...])
        gate = jax.nn.sigmoid(_dot(x2.astype(BF16), wg_ref[...]))
        pe = _dot(p_ref[r, :].astype(BF16), wp_ref[...])
        x3 = x2 + gate * pe
        o_ref[r, :] = x3
        if with_kv:
            kvn = _layer_norm(x3, kg_ref[...], kb_ref[...]).astype(BF16)
            kv = _dot(kvn, wkv_ref[...])
            w = k16_ref.shape[1]
            for h in range(w // SB_HEAD_DIM):
                k_ref[r, h, :] = kv[:, h * SB_HEAD_DIM:(h + 1) * SB_HEAD_DIM]
                v_ref[r, h, :] = kv[:, w + h * SB_HEAD_DIM:w + (h + 1) * SB_HEAD_DIM]
            k16_ref[r, :] = kv[:, :w].astype(BF16)
            v16_ref[r, :] = kv[:, w:].astype(BF16)

    x1 = [ln1(rows[0])]
    pre2 = []
    for s in range(row_groups):
        if s + 1 < row_groups:
            x1.append(ln1(rows[s + 1]))
        pre2.append(mlp(x1[s]))
        if s > 0:
            finish(pre2[s - 1], rows[s - 1])
    finish(pre2[-1], rows[-1])


def _post(xf, mf, pf, wo, g1, b1, w1, w2, g2, b2, wg, wp, kv_params, tm):
    n, d = xf.shape
    km, pd = mf.shape[1], pf.shape[1]
    with_kv = kv_params is not None
    row = lambda w: pl.BlockSpec((tm, w), lambda i: (i, 0))
    ins = [xf, mf, pf, wo, g1, b1, w1, w2, g2, b2, wg, wp]
    in_specs = [row(d), row(km), row(pd)] + [_const_spec(a.shape) for a in ins[3:]]
    out_specs = [row(d)]
    out_shape = [jax.ShapeDtypeStruct((n, d), F32)]
    if with_kv:
        ins += list(kv_params)
        in_specs += [_const_spec(a.shape) for a in kv_params]
        w = kv_params[2].shape[1] // 2
        n_sb = w // SB_HEAD_DIM
        out_specs += [pl.BlockSpec((tm, n_sb, SB_HEAD_DIM), lambda i: (i, 0, 0))] * 2 + [row(w)] * 2
        out_shape += ([jax.ShapeDtypeStruct((n, n_sb, SB_HEAD_DIM), F32)] * 2
                      + [jax.ShapeDtypeStruct((n, w), BF16)] * 2)
    return pl.pallas_call(
        functools.partial(_post_kernel, ff_chunk=min(FF_CHUNK, w1.shape[1]), with_kv=with_kv,
                          row_groups=1 if with_kv else ROW_GROUPS),
        grid=(n // tm,),
        in_specs=in_specs, out_specs=out_specs, out_shape=out_shape,
        compiler_params=pltpu.CompilerParams(dimension_semantics=("parallel",),
                                             vmem_limit_bytes=VMEM_LIMIT),
        name="post_kv" if with_kv else "post",
    )(*ins)


def _attn_kernel(x_ref, wq_ref, k_ref, v_ref, ut_ref, o_ref, q_s, suf_s, acc_s,
                 *, tq, q_offset, n_heads, head_dim, scale):
    KB = SB_BLOCK
    qi = pl.program_id(1)
    q = _dot(x_ref[...].astype(BF16), wq_ref[...]) * (scale * LOG2E)
    q_s[...] = q.astype(BF16)
    suf_s[...] = jnp.zeros_like(suf_s)
    acc_s[...] = jnp.zeros_like(acc_s)
    q_min = q_offset + qi * tq
    n_full = q_min // KB
    n_diag = -(-tq // KB)
    ut = ut_ref[...]
    heads = [slice(h * head_dim, (h + 1) * head_dim) for h in range(n_heads)]

    def key_block(kstart, masked, row_lo=0):
        rows = slice(row_lo, tq)
        if masked:
            qpos = q_min + row_lo + lax.broadcasted_iota(jnp.int32, (tq - row_lo, KB), 0)
            kpos = kstart + lax.broadcasted_iota(jnp.int32, (tq - row_lo, KB), 1)
            mask = kpos < qpos
        z2 = [_dot_nt(q_s[rows, hs], k_ref[pl.ds(kstart, KB), hs]) for hs in heads]
        sp2, lsig2 = [], []
        for z in z2:
            neg_abs = lax.bitcast_convert_type(lax.bitcast_convert_type(z, jnp.int32) | SIGN_BIT, F32)
            sp = jnp.maximum(z, 0.0) + jnp.log(1.0 + jnp.exp2(neg_abs)) * LOG2E
            lsig2.append(z - sp)
            if masked:
                sp = jnp.where(mask, sp, 0.0)
            sp2.append(sp.astype(BF16))
        it = [_dot(sp, ut) for sp in sp2]
        w = []
        for h in range(n_heads):
            wh = jnp.exp2(lsig2[h] - it[h][:, :KB] - suf_s[h, rows, :])
            if masked:
                wh = jnp.where(mask, wh, 0.0)
            w.append(wh.astype(BF16))
            suf_s[h, rows, :] = suf_s[h, rows, :] + it[h][:, KB:]
        for h, hs in enumerate(heads):
            acc_s[rows, hs] = acc_s[rows, hs] + _dot(w[h], v_ref[pl.ds(kstart, KB), hs])

    def any_row_alive():
        return (jnp.min(suf_s[...]) < UNDERFLOW_LOG2).astype(jnp.int32)

    for d in reversed(range(n_diag)):
        key_block(pl.multiple_of(q_min + d * KB, KB), True, row_lo=d * KB)

    def cond(carry):
        i, alive = carry
        return jnp.logical_and(i < n_full, alive > 0)

    def body(carry):
        i, _ = carry
        key_block(pl.multiple_of((n_full - 1 - i) * KB, KB), False)
        return i + 1, any_row_alive()

    lax.while_loop(cond, body, (jnp.int32(0), jnp.int32(1)))
    o_ref[...] = acc_s[...].astype(o_ref.dtype)


def _attn(x, wq, k16, v16, ut, q_offset, tq):
    bsz, t, d = x.shape
    tk, width = k16.shape[1], k16.shape[2]
    n_heads = width // SB_HEAD_DIM
    assert q_offset % SB_BLOCK == 0 and t % tq == 0 and (tq % SB_BLOCK == 0 or t == tq)
    assert tk % SB_BLOCK == 0 and tk >= q_offset + (-(-t // SB_BLOCK)) * SB_BLOCK
    kern = functools.partial(_attn_kernel, tq=tq, q_offset=q_offset, n_heads=n_heads,
                             head_dim=SB_HEAD_DIM, scale=SB_HEAD_DIM ** -0.5)
    return pl.pallas_call(
        kern,
        grid=(bsz, t // tq),
        in_specs=[pl.BlockSpec((None, tq, d), lambda b, i: (b, i, 0)),
                  _const_spec(wq.shape),
                  pl.BlockSpec((None, tk, width), lambda b, i: (b, 0, 0)),
                  pl.BlockSpec((None, tk, width), lambda b, i: (b, 0, 0)),
                  _const_spec(ut.shape)],
        out_specs=pl.BlockSpec((None, tq, width), lambda b, i: (b, i, 0)),
        out_shape=jax.ShapeDtypeStruct((bsz, t, width), BF16),
        scratch_shapes=[pltpu.VMEM((tq, width), BF16),
                        pltpu.VMEM((n_heads, tq, SB_BLOCK), F32),
                        pltpu.VMEM((tq, width), F32)],
        compiler_params=pltpu.CompilerParams(dimension_semantics=("parallel", "arbitrary"),
                                             vmem_limit_bytes=VMEM_LIMIT),
        name="sb_attn",
    )(x, wq, k16, v16, ut)


def _row(v):
    return v.reshape(1, -1).astype(F32)


def _pad_lanes(v):
    return jnp.pad(v.reshape(1, -1).astype(F32), ((0, 0), (0, LANES - v.size)))


def _prep_weights(prm):
    d_inner = prm['a_w_out'].shape[1]
    conv_dim = prm['a_conv_w'].shape[2]
    n_heads = prm['a_dt_bias'].shape[1]
    w = dict(prm)
    w['wz'] = prm['a_w_in'][:, :, :d_inner].astype(BF16)
    w['wx'] = prm['a_w_in'][:, :, d_inner:d_inner + conv_dim].astype(BF16)
    w['wd'] = jnp.pad(prm['a_w_in'][:, :, d_inner + conv_dim:], ((0, 0), (0, 0), (0, LANES - n_heads))).astype(BF16)
    for name in ('a_w_out', 'w_kv', 'b_w_q', 'b_w_out', 'mlp_w1', 'mlp_w2', 'ple_w', 'ple_gate_w'):
        w[name] = prm[name].astype(BF16)
    head_of_channel = jnp.arange(d_inner) // SSM_HEAD_DIM
    w['expand'] = (jnp.arange(LANES)[:, None] == head_of_channel[None, :]).astype(BF16)
    tt = jnp.arange(SSD_CHUNK)
    w['shift'] = jnp.concatenate([(tt[None, :] == tt[:, None] - s).astype(BF16) for s in range(CONV_W - 1, 0, -1)], axis=0)
    idx = jnp.arange(SB_BLOCK)
    upper = (idx[:, None] > idx[None, :]).astype(BF16)
    w['ut'] = jnp.concatenate([upper, jnp.ones((SB_BLOCK, SB_BLOCK), BF16)], axis=1)
    return w


def _run_trunk(x, p, conv_prev, ssm_prev, k_past, v_past, q_offset, w, tm, tq):
    bsz, t, d = x.shape
    n = bsz * t
    d_inner = w['a_w_out'].shape[1]
    xf = x.reshape(n, d)
    conv_states, ssm_states = [], []
    k_new = v_new = k16 = v16 = None
    t_pad = -(-t // SSD_CHUNK) * SSD_CHUNK
    for i in range(DEPTH):
        if i < N_A:
            z, xbc, dtr = _inproj(xf, w['wz'][i], w['wx'][i], w['wd'][i], min(INPROJ_ROWS, n))
            z, xbc, dtr = (a.reshape(bsz, t, -1) for a in (z, xbc, dtr))
            if t_pad != t:
                z, xbc, dtr = (jnp.pad(a, ((0, 0), (0, t_pad - t), (0, 0))) for a in (z, xbc, dtr))
            y, cnew, hlast = _ssd(
                xbc, z, dtr, conv_prev[i], ssm_prev[i].reshape(bsz, d_inner, D_STATE),
                w['a_conv_w'][i], _row(w['a_conv_b'][i]), _pad_lanes(w['a_dt_bias'][i]),
                _pad_lanes(w['a_A_log'][i]), _row(jnp.repeat(w['a_d_skip'][i], SSM_HEAD_DIM)),
                _row(w['a_norm_g'][i]), w['expand'], w['expand'].T, w['shift'], t)
            conv_states.append(cnew)
            ssm_states.append(hlast.reshape(bsz, d_inner // SSM_HEAD_DIM, SSM_HEAD_DIM, D_STATE))
            mf = y[:, :t].reshape(n, d_inner)
            wo = w['a_w_out'][i]
        else:
            j = i - N_A
            o = _attn(xf.reshape(bsz, t, d), w['b_w_q'][j], k16, v16, w['ut'], q_offset, tq)
            mf = o.reshape(n, -1)
            wo = w['b_w_out'][j]
        kv_params = (_row(w['kv_norm_g']), _row(w['kv_norm_b']), w['w_kv']) if i == N_A - 1 else None
        outs = _post(xf, mf, p[i].reshape(n, -1), wo, _row(w['ln1_g'][i]), _row(w['ln1_b'][i]),
                     w['mlp_w1'][i], w['mlp_w2'][i], _row(w['ln2_g'][i]), _row(w['ln2_b'][i]),
                     w['ple_gate_w'][i], w['ple_w'][i], kv_params, tm)
        xf = outs[0]
        if kv_params is not None:
            k_new, v_new, kb, vb = outs[1:]
            width = kb.shape[1]
            k16, v16 = kb.reshape(bsz, t, width), vb.reshape(bsz, t, width)
            if k_past is not None:
                k16 = jnp.concatenate([k_past.reshape(bsz, -1, width).astype(BF16), k16], axis=1)
                v16 = jnp.concatenate([v_past.reshape(bsz, -1, width).astype(BF16), v16], axis=1)
            tk = k16.shape[1]
            tk_pad = -(-tk // SB_BLOCK) * SB_BLOCK
            if tk_pad != tk:
                k16 = jnp.pad(k16, ((0, 0), (0, tk_pad - tk), (0, 0)))
                v16 = jnp.pad(v16, ((0, 0), (0, tk_pad - tk), (0, 0)))
    return (xf.reshape(bsz, t, d), jnp.stack(ssm_states), jnp.stack(conv_states),
            k_new.reshape((bsz, t) + k_new.shape[1:]), v_new.reshape((bsz, t) + v_new.shape[1:]))


def kernel(x_prompt, x_sample, p_prompt, p_sample, state_ssm, state_conv, cache_k, cache_v, a_w_in, a_conv_w, a_conv_b, a_dt_bias, a_A_log, a_d_skip, a_norm_g, a_w_out, kv_norm_g, kv_norm_b, w_kv, b_w_q, b_w_out, ln1_g, ln1_b, ln2_g, ln2_b, mlp_w1, mlp_w2, ple_w, ple_gate_w):
    prm = {
        'a_w_in': a_w_in, 'a_conv_w': a_conv_w, 'a_conv_b': a_conv_b, 'a_dt_bias': a_dt_bias,
        'a_A_log': a_A_log, 'a_d_skip': a_d_skip, 'a_norm_g': a_norm_g, 'a_w_out': a_w_out,
        'kv_norm_g': kv_norm_g, 'kv_norm_b': kv_norm_b, 'w_kv': w_kv,
        'b_w_q': b_w_q, 'b_w_out': b_w_out,
        'ln1_g': ln1_g, 'ln1_b': ln1_b, 'ln2_g': ln2_g, 'ln2_b': ln2_b,
        'mlp_w1': mlp_w1, 'mlp_w2': mlp_w2, 'ple_w': ple_w, 'ple_gate_w': ple_gate_w,
    }
    w = _prep_weights(prm)
    bp, tp, _ = x_prompt.shape
    bs, ts, _ = x_sample.shape
    n_heads = a_dt_bias.shape[1]
    conv0 = jnp.zeros((N_A, bp, CONV_W - 1, a_conv_w.shape[2]), F32)
    ssm0 = jnp.zeros((N_A, bp, n_heads, SSM_HEAD_DIM, D_STATE), F32)
    y_p, ssm_p, conv_p, k_p, v_p = _run_trunk(x_prompt, p_prompt, conv0, ssm0, None, None, 0, w,
                                              tm=min(512, bp * tp), tq=min(256, tp))
    y_s, ssm_s, conv_s, k_s, v_s = _run_trunk(x_sample, p_sample, state_conv, state_ssm, cache_k, cache_v,
                                              cache_k.shape[1], w, tm=min(512, bs * ts), tq=ts)
    return (y_p, y_s, ssm_p, conv_p, k_p, v_p, ssm_s, conv_s, k_s, v_s)
```
